```python
import math
import jax
import jax.numpy as jnp
from jax import lax
import numpy as np

D_MODEL = 1024
BATCH = 4
SEQ = 8192
DEPTH = 2

MIX_WIDTH = D_MODEL // 2
DN_HEADS = 4
DN_DIM = MIX_WIDTH // DN_HEADS
DN_CONV = 5
CHUNK = 64
POOL_WINDOWS = (2, 4, 8, 16)
POOL_GROUPS = 4
POOL_DIM = MIX_WIDTH // POOL_GROUPS
ML_HEADS = 4
ML_V_DIM = MIX_WIDTH // ML_HEADS
ML_QK_DIM = ML_V_DIM // 2
N_BRANCH = 3
N_EXPERTS = 32
N_GROUPS = 8
EXPERTS_PER_GROUP = N_EXPERTS // N_GROUPS
TOP_K = 2
D_EXPERT = D_MODEL // 2
MOE_BLOCK = 128
LN_EPS = 1e-5
NORM_EPS = 1e-6

SPLITS = (3 * MIX_WIDTH,
          MIX_WIDTH,
          2 * DN_HEADS,
          2 * DN_HEADS,
          MIX_WIDTH,
          ML_HEADS * ML_QK_DIM,
          ML_HEADS * ML_QK_DIM,
          MIX_WIDTH,
          MIX_WIDTH,
          2 * ML_HEADS,
          2 * ML_HEADS,
          N_BRANCH * D_MODEL)
D_IN_PROJ = sum(SPLITS)

kernel_name = 'hybrid_gdn_pool_mlstm_grouped_moe_encoder'


def layer_norm(x, g, b):
    xf = x.astype(jnp.float32)
    mu = jnp.mean(xf, -1, keepdims=True)
    var = jnp.mean(jnp.square(xf - mu), -1, keepdims=True)
    return ((xf - mu) * lax.rsqrt(var + LN_EPS) * g.astype(jnp.float32) + b.astype(jnp.float32)).astype(x.dtype)


def head_rms_norm(h, w):
    h = h * lax.rsqrt(jnp.mean(h * h, -1, keepdims=True) + NORM_EPS)
    return h.reshape(h.shape[0], h.shape[1], -1) * w


def head_layer_norm(h, w):
    h = h - jnp.mean(h, -1, keepdims=True)
    h = h * lax.rsqrt(jnp.mean(h * h, -1, keepdims=True) + NORM_EPS)
    return h.reshape(h.shape[0], h.shape[1], -1) * w


def l2norm(x):
    return x * lax.rsqrt(jnp.sum(x * x, -1, keepdims=True) + NORM_EPS)


def centred_depthwise_conv(x, w):
    width, ch = w.shape
    pad = width // 2
    return lax.conv_general_dilated(x, w[:, None, :], window_strides=(1,), padding=[(pad, pad)],
                                    dimension_numbers=('NWC', 'WIO', 'NWC'), feature_group_count=ch)


def tri_masks(n):
    ones = jnp.ones((n, n), dtype=bool)
    return jnp.tril(ones), jnp.tril(ones, -1)


def gated_delta_rule(q, k, v, beta, g):
    nb, nh, ns, dk = q.shape
    dv = v.shape[-1]
    nc = ns // CHUNK
    q, k, v = (t.reshape(nb, nh, nc, CHUNK, -1) for t in (q, k, v))
    beta, g = (t.reshape(nb, nh, nc, CHUNK) for t in (beta, g))
    incl, strict = tri_masks(CHUNK)
    gc = jnp.cumsum(g, axis=-1)
    decay = jnp.exp(jnp.where(incl, gc[..., :, None] - gc[..., None, :], -jnp.inf))
    kb = k * beta[..., None]
    m = jnp.where(strict, jnp.einsum('bhnid,bhnjd->bhnij', kb, k) * decay, 0.0)
    a_mat = m + jnp.eye(CHUNK, dtype=m.dtype)
    rhs = jnp.concatenate([v * beta[..., None], kb * jnp.exp(gc)[..., None]], axis=-1)
    sol = lax.linalg.triangular_solve(a_mat, rhs, left_side=True, lower=True, unit_diagonal=True)
    u, w = sol[..., :dv], sol[..., dv:]
    g_last = gc[..., -1]
    k_dec = k * jnp.exp(g_last[..., None] - gc)[..., None]

    def step(state, inp):
        w_n, u_n, kd_n, gl_n = inp
        v_new = u_n - jnp.einsum('bhlk,bhkv->bhlv', w_n, state)
        state_next = state * jnp.exp(gl_n)[..., None, None] + jnp.einsum('bhlk,bhlv->bhkv', kd_n, v_new)
        return state_next, (state, v_new)

    xs = tuple(jnp.moveaxis(t, 2, 0) for t in (w, u, k_dec, g_last))
    s0 = jnp.zeros((nb, nh, dk, dv), jnp.float32)
    _, (states, v_new) = lax.scan(step, s0, xs)
    states = jnp.moveaxis(states, 0, 2)
    v_new = jnp.moveaxis(v_new, 0, 2)
    attn = jnp.einsum('bhnid,bhnjd->bhnij', q, k) * decay
    o = (jnp.einsum('bhnld,bhndv->bhnlv', q * jnp.exp(gc)[..., None], states)
         + jnp.einsum('bhnij,bhnjv->bhniv', attn, v_new))
    return o.reshape(nb, nh, ns, dv)


def mlstm_chunkwise(q, k, v, i_pre, f_pre):
    nb, nh, ns, dqk = q.shape
    dv = v.shape[-1]
    nc = ns // CHUNK
    q, k, v = (t.reshape(nb, nh, nc, CHUNK, -1) for t in (q, k, v))
    i_pre, f_pre = (t.reshape(nb, nh, nc, CHUNK) for t in (i_pre, f_pre))
    incl, _ = tri_masks(CHUNK)
    bc = jnp.cumsum(jax.nn.log_sigmoid(f_pre), axis=-1)
    b_last = bc[..., -1]
    d_log = jnp.where(incl, bc[..., :, None] - bc[..., None, :] + i_pre[..., None, :], -jnp.inf)
    lw = b_last[..., None] - bc + i_pre
    a_max = jnp.max(lw, -1)
    wgt = jnp.exp(lw - a_max[..., None])
    c_chunk = jnp.einsum('bhnl,bhnld,bhnle->bhnde', wgt, k, v)
    n_chunk = jnp.einsum('bhnl,bhnld->bhnd', wgt, k)

    def step(carry, inp):
        c, nv, m = carry
        cc, ncv, am, bl = inp
        m_new = jnp.maximum(bl + m, am)
        sp = jnp.exp(bl + m - m_new)
        sc = jnp.exp(am - m_new)
        c_new = c * sp[..., None, None] + cc * sc[..., None, None]
        n_new = nv * sp[..., None] + ncv * sc[..., None]
        return (c_new, n_new, m_new), (c, nv, m)

    init = (jnp.zeros((nb, nh, dqk, dv), jnp.float32), jnp.zeros((nb, nh, dqk), jnp.float32),
            jnp.zeros((nb, nh), jnp.float32))
    xs = tuple(jnp.moveaxis(t, 2, 0) for t in (c_chunk, n_chunk, a_max, b_last))
    _, (c_prev, n_prev, m_prev) = lax.scan(step, init, xs)
    c_prev, n_prev, m_prev = (jnp.moveaxis(t, 0, 2) for t in (c_prev, n_prev, m_prev))
    inter_log = bc + m_prev[..., None]
    m_out = jnp.maximum(inter_log, jnp.max(d_log, -1))
    s = jnp.einsum('bhnid,bhnjd->bhnij', q, k) * jnp.exp(d_log - m_out[..., None])
    inter = jnp.exp(inter_log - m_out)
    num = (inter[..., None] * jnp.einsum('bhnld,bhnde->bhnle', q, c_prev)
           + jnp.einsum('bhnij,bhnje->bhnie', s, v))
    den = inter * jnp.einsum('bhnld,bhnd->bhnl', q, n_prev) + jnp.sum(s, -1)
    h = num / jnp.maximum(jnp.abs(den), jnp.exp(-m_out))[..., None]
    return h.reshape(nb, nh, ns, dv)


def bidirectional(fn, shared, fwd_gates, bwd_gates):
    flip = lambda t: jnp.flip(t, axis=2)
    y_f = fn(*shared, *fwd_gates)
    y_b = fn(*[flip(t) for t in shared], *[flip(t) for t in bwd_gates])
    return y_f + flip(y_b)


def multiscale_pool(u, w_group, scale):
    nb, ns, _ = u.shape
    cs = jnp.concatenate([jnp.zeros((nb, 1, MIX_WIDTH), u.dtype), jnp.cumsum(u, axis=1)], axis=1)
    pos = jnp.arange(ns)
    outs = []
    for gi, win in enumerate(POOL_WINDOWS):
        lo = jnp.clip(pos - win // 2, 0, ns)
        hi = jnp.clip(pos + win // 2, 0, ns)
        cg = cs[:, :, gi * POOL_DIM:(gi + 1) * POOL_DIM]
        mean = (cg[:, hi] - cg[:, lo]) / (hi - lo).astype(u.dtype)[None, :, None]
        outs.append(mean - u[:, :, gi * POOL_DIM:(gi + 1) * POOL_DIM])
    p = jnp.stack(outs, axis=2)
    y = jnp.einsum('bsgc,gcd->bsgd', p, w_group).reshape(nb, ns, MIX_WIDTH)
    return y * scale


def hybrid_mixer(h, w_in, dn_conv_w, dn_a_log, dn_dt_bias, dn_norm_w, pool_w, pool_scale,
                 ml_gate_b, ml_norm_w, w_branch, w_out):
    nb, ns, _ = h.shape
    f32 = jnp.float32
    idx = [int(c) for c in np.cumsum(SPLITS)[:-1]]
    proj = h @ w_in
    (dn_qkv, dn_z, dn_a, dn_b, pool_u, ml_q, ml_k, ml_v, ml_o, ml_i, ml_f,
     gate_pre) = jnp.split(proj, idx, axis=-1)
    heads = lambda t, nh: t.astype(f32).reshape(nb, ns, nh, -1).transpose(0, 2, 1, 3)
    dir_gates = lambda t: t.astype(f32).reshape(nb, ns, 2, -1).transpose(2, 0, 3, 1)

    qkv = jax.nn.silu(centred_depthwise_conv(dn_qkv.astype(f32), dn_conv_w.astype(f32)))
    dq, dk, dv = jnp.split(qkv, 3, axis=-1)
    dq = l2norm(heads(dq, DN_HEADS)) * DN_DIM ** -0.5
    dk = l2norm(heads(dk, DN_HEADS))
    dv = heads(dv, DN_HEADS)
    beta = jax.nn.sigmoid(dir_gates(dn_b))
    a_log = dn_a_log.astype(f32).reshape(2, 1, DN_HEADS, 1)
    dt_bias = dn_dt_bias.astype(f32).reshape(2, 1, DN_HEADS, 1)
    g = -jnp.exp(a_log) * jax.nn.softplus(dir_gates(dn_a) + dt_bias)
    o_dn = bidirectional(gated_delta_rule, (dq, dk, dv), (beta[0], g[0]), (beta[1], g[1]))
    y_dn = head_rms_norm(o_dn.transpose(0, 2, 1, 3), dn_norm_w.astype(f32)) * jax.nn.silu(dn_z.astype(f32))

    y_pool = multiscale_pool(pool_u.astype(f32), pool_w.astype(f32), pool_scale.astype(f32))

    mq = heads(ml_q, ML_HEADS)
    mk = heads(ml_k, ML_HEADS) * ML_QK_DIM ** -0.5
    mv = heads(ml_v, ML_HEADS)
    gb = ml_gate_b.astype(f32).reshape(2, 2, 1, ML_HEADS, 1)
    ig = dir_gates(ml_i) + gb[0]
    fg = dir_gates(ml_f) + gb[1]
    h_ml = bidirectional(mlstm_chunkwise, (mq, mk, mv), (ig[0], fg[0]), (ig[1], fg[1]))
    y_ml = head_layer_norm(h_ml.transpose(0, 2, 1, 3), ml_norm_w.astype(f32)) * jax.nn.sigmoid(ml_o.astype(f32))

    branches = jnp.stack([y_dn, y_pool, y_ml], axis=2).astype(h.dtype)
    y = jnp.einsum('bsrm,rmd->bsrd', branches, w_branch)
    gate = jax.nn.sigmoid(gate_pre.reshape(nb, ns, N_BRANCH, D_MODEL))
    merged = jnp.sum(gate * y, axis=2)
    return merged @ w_out


def route(h2, router_w, router_b):
    t = h2.shape[0]
    scores = jax.nn.sigmoid(h2.astype(jnp.float32) @ router_w.astype(jnp.float32))
    sel = (scores + router_b.astype(jnp.float32)).reshape(t, N_GROUPS, EXPERTS_PER_GROUP)
    group_score = jnp.sum(lax.top_k(sel, TOP_K)[0], axis=-1)
    grp = jnp.argmax(group_score, axis=-1)
    sel_in = sel[jnp.arange(t), grp]
    _, local = lax.top_k(sel_in, TOP_K)
    expert = grp[:, None] * EXPERTS_PER_GROUP + local
    w = jnp.take_along_axis(scores, expert, axis=1)
    return expert, w / jnp.sum(w, axis=-1, keepdims=True)


def moe_ffn(h, router_w, router_b, w1, w3, w2):
    nb, ns, d = h.shape
    t = nb * ns
    h2 = h.reshape(t, d)
    expert, gate = route(h2, router_w, router_b)
    a = t * TOP_K
    flat_e = expert.reshape(a)
    flat_tok = jnp.repeat(jnp.arange(t, dtype=jnp.int32), TOP_K)
    flat_w = gate.reshape(a)
    order = jnp.argsort(flat_e)
    e_sorted = flat_e[order]
    counts = jnp.bincount(flat_e, length=N_EXPERTS)
    padded = ((counts + MOE_BLOCK - 1) // MOE_BLOCK) * MOE_BLOCK
    start = jnp.cumsum(counts) - counts
    pend = jnp.cumsum(padded)
    pstart = pend - padded
    dest = pstart[e_sorted] + jnp.arange(a) - start[e_sorted]
    n_rows = a + N_EXPERTS * MOE_BLOCK
    n_blocks = n_rows // MOE_BLOCK
    row_tok = jnp.full((n_rows,), t, dtype=jnp.int32).at[dest].set(flat_tok[order])
    row_w = jnp.zeros((n_rows,), jnp.float32).at[dest].set(flat_w[order])
    block_expert = jnp.clip(jnp.searchsorted(pend, jnp.arange(n_blocks) * MOE_BLOCK, side='right'),
                            0, N_EXPERTS - 1)
    hp = jnp.concatenate([h2, jnp.zeros((1, d), h2.dtype)], axis=0)
    xb = hp[row_tok].reshape(n_blocks, MOE_BLOCK, d)

    def expert_block(args):
        xblk, e = args
        act = jax.nn.silu(xblk @ w1[e]) * (xblk @ w3[e])
        return act @ w2[e]

    yb = lax.map(expert_block, (xb, block_expert))
    y = yb.reshape(n_rows, d).astype(jnp.float32) * row_w[:, None]
    out = jnp.zeros((t + 1, d), jnp.float32).at[row_tok].add(y)[:t]
    return out.reshape(nb, ns, d).astype(h.dtype)


def setup_inputs(seed: int = 0) -> dict:
    key = jax.random.key(seed)
    ks = jax.random.split(key, 24)
    f32 = jnp.float32
    nrm = lambda i, shape, scale: jax.random.normal(ks[i], shape, f32) * scale
    out_scale = (8.0 * DEPTH) ** -0.25
    dt = jnp.exp(jax.random.uniform(ks[4], (DEPTH, 2 * DN_HEADS), f32, math.log(1e-3), math.log(1e-1)))
    dn_dt_bias = dt + jnp.log(-jnp.expm1(-dt))
    dn_a_log = jnp.log(jax.random.uniform(ks[5], (DEPTH, 2 * DN_HEADS), f32, 1.0, 16.0))
    f_bias = jnp.tile(jnp.linspace(3.0, 6.0, ML_HEADS, dtype=f32), (DEPTH, 2)) + nrm(6, (DEPTH, 2 * ML_HEADS), 0.1)
    i_bias = nrm(7, (DEPTH, 2 * ML_HEADS), 0.1)
    return {
        'x': nrm(0, (BATCH, SEQ, D_MODEL), 1.0),
        'ln_in_g': 1.0 + nrm(1, (D_MODEL,), 0.02),
        'ln_in_b': nrm(2, (D_MODEL,), 0.02),
        'w_in': nrm(3, (DEPTH, D_MODEL, D_IN_PROJ), D_MODEL ** -0.5),
        'dn_conv_w': nrm(8, (DEPTH, DN_CONV, 3 * MIX_WIDTH), DN_CONV ** -0.5),
        'dn_a_log': dn_a_log,
        'dn_dt_bias': dn_dt_bias,
        'dn_norm_w': 1.0 + nrm(9, (DEPTH, MIX_WIDTH), 0.02),
        'pool_w': nrm(10, (DEPTH, POOL_GROUPS, POOL_DIM, POOL_DIM), POOL_DIM ** -0.5),
        'pool_scale': 1.0 + nrm(11, (DEPTH, MIX_WIDTH), 0.02),
        'ml_gate_b': jnp.concatenate([i_bias, f_bias], axis=-1),
        'ml_norm_w': 1.0 + nrm(12, (DEPTH, MIX_WIDTH), 0.02),
        'w_branch': nrm(13, (DEPTH, N_BRANCH, MIX_WIDTH, D_MODEL), MIX_WIDTH ** -0.5),
        'w_out': nrm(14, (DEPTH, D_MODEL, D_MODEL), out_scale * D_MODEL ** -0.5),
        'ln_mix_g': 1.0 + nrm(15, (DEPTH, D_MODEL), 0.02),
        'ln_mix_b': nrm(16, (DEPTH, D_MODEL), 0.02),
        'router_w': nrm(17, (D_MODEL, N_EXPERTS), D_MODEL ** -0.5),
        'router_b': nrm(18, (N_EXPERTS,), 0.01),
        'moe_w1': nrm(19, (DEPTH, N_EXPERTS, D_MODEL, D_EXPERT), D_MODEL ** -0.5),
        'moe_w3': nrm(20, (DEPTH, N_EXPERTS, D_MODEL, D_EXPERT), D_MODEL ** -0.5),
        'moe_w2': nrm(21, (DEPTH, N_EXPERTS, D_EXPERT, D_MODEL), out_scale * D_EXPERT ** -0.5),
        'ln_ffn_g': 1.0 + nrm(22, (DEPTH, D_MODEL), 0.02),
        'ln_ffn_b': nrm(23, (DEPTH, D_MODEL), 0.02),
    }


def reference(x, ln_in_g, ln_in_b, w_in, dn_conv_w, dn_a_log, dn_dt_bias, dn_norm_w, pool_w, pool_scale,
              ml_gate_b, ml_norm_w, w_branch, w_out, ln_mix_g, ln_mix_b, router_w, router_b,
              moe_w1, moe_w3, moe_w2, ln_ffn_g, ln_ffn_b):
    alpha = (2.0 * DEPTH) ** 0.25
    h = layer_norm(x, ln_in_g, ln_in_b)
    for l in range(DEPTH):
        mix = hybrid_mixer(h, w_in[l], dn_conv_w[l], dn_a_log[l], dn_dt_bias[l], dn_norm_w[l],
                           pool_w[l], pool_scale[l], ml_gate_b[l], ml_norm_w[l], w_branch[l], w_out[l])
        h = layer_norm(alpha * h + mix, ln_mix_g[l], ln_mix_b[l])
        ffn = moe_ffn(h, router_w, router_b, moe_w1[l], moe_w3[l], moe_w2[l])
        h = layer_norm(alpha * h + ffn, ln_ffn_g[l], ln_ffn_b[l])
    return h
```

```python
import functools
import math

import jax
import jax.numpy as jnp
import numpy as np
from jax import lax
from jax.experimental import pallas as pl
from jax.experimental.pallas import tpu as pltpu

F32 = jnp.float32
BF16 = jnp.bfloat16

D_MODEL = 1024
DEPTH = 2
MIX_WIDTH = D_MODEL // 2
DN_HEADS = 4
DN_DIM = MIX_WIDTH // DN_HEADS
DN_CONV = 5
CHUNK = 64
POOL_WINDOWS = (2, 4, 8, 16)
POOL_DIM = MIX_WIDTH // len(POOL_WINDOWS)
ML_HEADS = 4
ML_V_DIM = MIX_WIDTH // ML_HEADS
ML_QK_DIM = ML_V_DIM // 2
N_BRANCH = 3
N_EXPERTS = 32
N_GROUPS = 8
EXPERTS_PER_GROUP = N_EXPERTS // N_GROUPS
TOP_K = 2
D_EXPERT = D_MODEL // 2
LN_EPS = 1e-5
NORM_EPS = 1e-6
ALPHA = (2.0 * DEPTH) ** 0.25

LANES = 128
HALO = 16
SEQ_BLOCK = 512
ROW_BLOCK = 256
MOE_ROWS = 256
NEG_BIG = -1e30
VMEM_LIMIT = 56 * 1024 * 1024

_SPLITS = (3 * MIX_WIDTH, MIX_WIDTH, 2 * DN_HEADS, 2 * DN_HEADS, MIX_WIDTH, ML_HEADS * ML_QK_DIM,
           ML_HEADS * ML_QK_DIM, MIX_WIDTH, MIX_WIDTH, 2 * ML_HEADS, 2 * ML_HEADS, N_BRANCH * D_MODEL)
_OFF = [0] + [int(c) for c in np.cumsum(_SPLITS)]
MAIN_WIDTH = 8 * MIX_WIDTH


def _column_orders():
    o = _OFF
    main = list(range(o[0], o[1])) + list(range(o[1], o[2])) + list(range(o[4], o[5]))
    for h in range(ML_HEADS):
        main += list(range(o[5] + h * ML_QK_DIM, o[5] + (h + 1) * ML_QK_DIM))
        main += list(range(o[6] + h * ML_QK_DIM, o[6] + (h + 1) * ML_QK_DIM))
    main += list(range(o[7], o[8])) + list(range(o[8], o[9]))
    small = list(range(o[2], o[4])) + list(range(o[9], o[11]))
    gate = list(range(o[11], o[12]))
    return np.asarray(main), np.asarray(small), np.asarray(gate)


def _cparams(sem, vmem=VMEM_LIMIT):
    return pltpu.CompilerParams(dimension_semantics=sem, vmem_limit_bytes=vmem)


def _dot(a, b):
    return jnp.dot(a, b, preferred_element_type=F32)


def _dot_nt(a, b):
    return lax.dot_general(a, b, (((1,), (1,)), ((), ())), preferred_element_type=F32)


def _dot_tn(a, b):
    return lax.dot_general(a, b, (((0,), (0,)), ((), ())), preferred_element_type=F32)


def _split3(a):
    a1 = a.astype(BF16)
    r1 = a - a1.astype(F32)
    a2 = r1.astype(BF16)
    a3 = (r1 - a2.astype(F32)).astype(BF16)
    return a1, a2, a3


def _dot_x3(a, b):
    a1 = a.astype(BF16)
    a2 = (a - a1.astype(F32)).astype(BF16)
    b1 = b.astype(BF16)
    b2 = (b - b1.astype(F32)).astype(BF16)
    return _dot(a1, b1) + (_dot(a1, b2) + _dot(a2, b1))


def _dot_exact_lhs(a, b01):
    a1, a2, a3 = _split3(a)
    return _dot(a1, b01) + (_dot(a2, b01) + _dot(a3, b01))


def _sigmoid(x):
    return 1.0 / (1.0 + jnp.exp(-x))


def _softplus(x):
    return jnp.maximum(x, 0.0) + jnp.log(1.0 + jnp.exp(-jnp.abs(x)))


def _iota2(shape, dim):
    return lax.broadcasted_iota(jnp.int32, shape, dim)


def _ln_rows(x, g, b):
    mu = jnp.mean(x, -1, keepdims=True)
    xc = x - mu
    var = jnp.mean(xc * xc, -1, keepdims=True)
    return xc * lax.rsqrt(var + LN_EPS) * g + b


def _ln_kernel(x_ref, g_ref, b_ref, o_ref, obf_ref):
    y = _ln_rows(x_ref[...], g_ref[...], b_ref[...])
    o_ref[...] = y
    obf_ref[...] = y.astype(BF16)


def _layer_norm_in(x2, g, b):
    t, d = x2.shape
    tm = ROW_BLOCK
    row = pl.BlockSpec((tm, d), lambda i: (i, 0))
    vec = pl.BlockSpec((1, d), lambda i: (0, 0))
    return pl.pallas_call(
        _ln_kernel, grid=(t // tm,), in_specs=[row, vec, vec], out_specs=[row, row],
        out_shape=[jax.ShapeDtypeStruct((t, d), F32), jax.ShapeDtypeStruct((t, d), BF16)],
        compiler_params=_cparams(("parallel",)), name="ln_in",
    )(x2, g.reshape(1, d), b.reshape(1, d))


def _matmul_kernel(x_ref, w_ref, o_ref):
    o_ref[...] = _dot(x_ref[...], w_ref[...]).astype(o_ref.dtype)


def _matmul(x, w, out_dtype, tm, tn, name):
    t, k = x.shape
    n = w.shape[1]
    return pl.pallas_call(
        _matmul_kernel, grid=(n // tn, t // tm),
        in_specs=[pl.BlockSpec((tm, k), lambda j, i: (i, 0)), pl.BlockSpec((k, tn), lambda j, i: (0, j))],
        out_specs=pl.BlockSpec((tm, tn), lambda j, i: (i, j)),
        out_shape=jax.ShapeDtypeStruct((t, n), out_dtype),
        compiler_params=_cparams(("parallel", "parallel")), name=name,
    )(x, w)


def _fill_halo(xs_ref, main, prev, nxt, i, nblk):
    ts = main.shape[0]
    xs_ref[0:8, :] = jnp.where(i > 0, prev[HALO - 8:HALO, :], 0.0)
    xs_ref[8:8 + ts, :] = main
    xs_ref[8 + ts:16 + ts, :] = jnp.where(i < nblk - 1, nxt[0:8, :], 0.0)


def _dn_prep_kernel(x_ref, xp_ref, xn_ref, cw_ref, o_ref, xs_ref):
    i = pl.program_id(1)
    j = pl.program_id(2)
    ts = x_ref.shape[1]
    _fill_halo(xs_ref, x_ref[0].astype(F32), xp_ref[0].astype(F32), xn_ref[0].astype(F32), i, pl.num_programs(1))
    pad = DN_CONV // 2
    acc = jnp.zeros((ts, LANES), F32)
    for w in range(DN_CONV):
        acc = acc + xs_ref[8 + w - pad:8 + w - pad + ts, :] * cw_ref[w:w + 1, :]
    y = acc * _sigmoid(acc)
    inv = lax.rsqrt(jnp.sum(y * y, -1, keepdims=True) + NORM_EPS)
    scale = jnp.where(j < DN_HEADS, inv * DN_DIM ** -0.5, jnp.where(j < 2 * DN_HEADS, inv, 1.0))
    o_ref[0] = (y * scale).astype(BF16)


def _halo_specs(ts, width, col_of):
    r = ts // HALO

    def main(b, i, *rest):
        return (b, i, col_of(*rest))

    def prev(b, i, *rest):
        return (b, jnp.maximum(i * r - 1, 0), col_of(*rest))

    def make_next(nrow_blocks):
        def nxt(b, i, *rest):
            return (b, jnp.minimum((i + 1) * r, nrow_blocks - 1), col_of(*rest))
        return nxt

    return main, prev, make_next


def _dn_prep(proj3, conv_w):
    nb, ns, _ = proj3.shape
    ts = SEQ_BLOCK
    main, prev, make_next = _halo_specs(ts, LANES, lambda j: j)
    nxt = make_next(ns // HALO)
    ncol = 3 * DN_HEADS
    return pl.pallas_call(
        _dn_prep_kernel, grid=(nb, ns // ts, ncol),
        in_specs=[pl.BlockSpec((1, ts, LANES), main), pl.BlockSpec((1, HALO, LANES), prev),
                  pl.BlockSpec((1, HALO, LANES), nxt), pl.BlockSpec((DN_CONV, LANES), lambda b, i, j: (0, j))],
        out_specs=pl.BlockSpec((1, ts, LANES), main),
        out_shape=jax.ShapeDtypeStruct((nb, ns, 3 * MIX_WIDTH), BF16),
        scratch_shapes=[pltpu.VMEM((ts + 16, LANES), F32)],
        compiler_params=_cparams(("parallel", "parallel", "parallel")), name="dn_prep",
    )(proj3, proj3, proj3, conv_w)


def _chunk_masks(rev):
    r = _iota2((CHUNK, CHUNK), 0)
    c = _iota2((CHUNK, CHUNK), 1)
    if rev:
        return r <= c, r < c, r >= c
    return r >= c, r > c, r <= c


def _col_bcast(row, width=LANES):
    r = _iota2((CHUNK, CHUNK), 0)
    c = _iota2((CHUNK, CHUNK), 1)
    diag = jnp.where(r == c, jnp.broadcast_to(row, (CHUNK, CHUNK)), 0.0)
    return _dot_exact_lhs(diag, jnp.ones((CHUNK, width), BF16))


def _tri_inverse(m):
    r = _iota2((CHUNK, CHUNK), 0)
    c = _iota2((CHUNK, CHUNK), 1)
    same16 = (r // 16) == (c // 16)
    same32 = (r // 32) == (c // 32)
    eye = jnp.where(r == c, 1.0, 0.0)
    md = jnp.where(same16, m, 0.0)
    p = eye - md
    a = _dot_x3(md, md)
    p = p + _dot_x3(p, a)
    a = _dot_x3(a, a)
    p = p + _dot_x3(p, a)
    a = _dot_x3(a, a)
    p = p + _dot_x3(p, a)
    c1 = jnp.where(same32, m - md, 0.0)
    p = p - _dot_x3(_dot_x3(p, c1), p)
    c2 = jnp.where(same32, 0.0, m)
    p = p - _dot_x3(_dot_x3(p, c2), p)
    return p


def _gdn_chunk(q, k, v, a_row, b_row, a_log, dt_bias, state, rev):
    incl, strict, cum = _chunk_masks(rev)
    cum01 = jnp.where(cum, 1.0, 0.0).astype(BF16)
    g_row = -jnp.exp(jnp.full((1, CHUNK), a_log, F32)) * _softplus(a_row + dt_bias)
    beta_row = _sigmoid(b_row)
    gc_row = _dot_exact_lhs(g_row, cum01)
    g_tot = _dot_exact_lhs(g_row, jnp.ones((CHUNK, LANES), BF16))
    gc_col = _col_bcast(gc_row)
    beta_col = _col_bcast(beta_row)
    decay = jnp.exp(jnp.where(incl, gc_col[:, :CHUNK] - gc_row, NEG_BIG))
    kf = k.astype(F32)
    kb = kf * beta_col
    m = jnp.where(strict, _dot_nt(kb.astype(BF16), k) * decay, 0.0)
    t_inv = _tri_inverse(m)
    e_col = jnp.exp(gc_col)
    rhs = jnp.concatenate([v.astype(F32) * beta_col, kb * e_col], axis=1)
    sol = _dot_x3(t_inv, rhs)
    u = sol[:, :DN_DIM]
    w = sol[:, DN_DIM:]
    k_dec = kf * jnp.exp(g_tot - gc_col)
    s_bf = state.astype(BF16)
    v_new = u - _dot(w.astype(BF16), s_bf)
    v_new_bf = v_new.astype(BF16)
    new_state = state * jnp.exp(g_tot) + _dot_tn(k_dec.astype(BF16), v_new_bf)
    attn = _dot_nt(q, k) * decay
    o = _dot((q.astype(F32) * e_col).astype(BF16), s_bf) + _dot(attn.astype(BF16), v_new_bf)
    return o, new_state


def _gdn_kernel(par_ref, qf_ref, kf_ref, vf_ref, qb_ref, kb_ref, vb_ref, gf_ref, gb_ref,
                of_ref, ob_ref, sf_ref, sb_ref):
    h = pl.program_id(1)
    i = pl.program_id(2)
    nch = qf_ref.shape[1] // CHUNK

    @pl.when(i == 0)
    def _():
        sf_ref[...] = jnp.zeros_like(sf_ref)
        sb_ref[...] = jnp.zeros_like(sb_ref)

    def one(q_ref, k_ref, v_ref, g_ref, o_ref, s_ref, c, ch, rev):
        r0 = pl.multiple_of(c * CHUNK, CHUNK)
        rows = pl.ds(r0, CHUNK)
        a_row = g_ref[0, c, pl.ds(ch, 1), :]
        b_row = g_ref[0, c, pl.ds(2 * DN_HEADS + ch, 1), :]
        o, s_new = _gdn_chunk(q_ref[0, rows, :], k_ref[0, rows, :], v_ref[0, rows, :], a_row, b_row,
                              par_ref[ch], par_ref[2 * DN_HEADS + ch], s_ref[...], rev)
        o_ref[0, rows, :] = o
        s_ref[...] = s_new

    def body(c, carry):
        one(qf_ref, kf_ref, vf_ref, gf_ref, of_ref, sf_ref, c, h, False)
        one(qb_ref, kb_ref, vb_ref, gb_ref, ob_ref, sb_ref, nch - 1 - c, DN_HEADS + h, True)
        return carry

    lax.fori_loop(0, nch, body, 0)


def _gdn(params, qkv, gates_t):
    nb, ns, _ = qkv.shape
    ts = SEQ_BLOCK
    nblk = ns // ts
    nch = ts // CHUNK
    H = DN_HEADS

    def col(off, reverse):
        if reverse:
            return lambda b, h, i: (b, nblk - 1 - i, off + h)
        return lambda b, h, i: (b, i, off + h)

    blk = lambda off, reverse: pl.BlockSpec((1, ts, LANES), col(off, reverse))
    gspec = lambda reverse: pl.BlockSpec(
        (1, nch, 32, CHUNK), (lambda b, h, i: (b, nblk - 1 - i, 0, 0)) if reverse else (lambda b, h, i: (b, i, 0, 0)))
    out_sd = jax.ShapeDtypeStruct((nb, ns, MIX_WIDTH), F32)
    return pl.pallas_call(
        _gdn_kernel, grid=(nb, H, nblk),
        in_specs=[pl.BlockSpec(memory_space=pltpu.SMEM),
                  blk(0, False), blk(H, False), blk(2 * H, False),
                  blk(0, True), blk(H, True), blk(2 * H, True),
                  gspec(False), gspec(True)],
        out_specs=[pl.BlockSpec((1, ts, LANES), col(0, False)), pl.BlockSpec((1, ts, LANES), col(0, True))],
        out_shape=[out_sd, out_sd],
        scratch_shapes=[pltpu.VMEM((DN_DIM, DN_DIM), F32), pltpu.VMEM((DN_DIM, DN_DIM), F32)],
        compiler_params=_cparams(("parallel", "parallel", "arbitrary")), name="gdn",
    )(params, qkv, qkv, qkv, qkv, qkv, qkv, gates_t, gates_t)


def _mlstm_chunk(q, k, v_ext, i_row, f_row, c_state, m_state, rev):
    incl, _, cum = _chunk_masks(rev)
    cum01 = jnp.where(cum, 1.0, 0.0).astype(BF16)
    logf = -_softplus(-f_row)
    bc_row = _dot_exact_lhs(logf, cum01)
    b_tot = _dot_exact_lhs(logf, jnp.ones((CHUNK, LANES), BF16))
    lw_row = b_tot[:, :CHUNK] - bc_row + i_row
    a_max = jnp.max(lw_row, axis=-1, keepdims=True)
    wgt_row = jnp.exp(lw_row - a_max)
    wgt_col = _col_bcast(wgt_row, CHUNK)
    bc_col = _col_bcast(bc_row)
    cc = _dot_tn((k.astype(F32) * wgt_col).astype(BF16), v_ext)
    m_new = jnp.maximum(b_tot + m_state, a_max)
    sp = jnp.exp(b_tot + m_state - m_new)
    sc = jnp.exp(a_max - m_new)
    c_new = c_state * jnp.concatenate([sp, sp], axis=1) + cc * jnp.concatenate([sc, sc], axis=1)
    inter_log = bc_col + m_state
    d_log = jnp.where(incl, bc_col[:, :CHUNK] - bc_row + i_row, NEG_BIG)
    m_out = jnp.maximum(inter_log, jnp.max(d_log, axis=-1, keepdims=True))
    s = _dot_nt(q, k) * jnp.exp(d_log - m_out[:, :CHUNK])
    inter = jnp.exp(inter_log - m_out)
    inter2 = jnp.concatenate([inter, inter], axis=1)
    num = inter2 * _dot(q, c_state.astype(BF16)) + _dot(s.astype(BF16), v_ext)
    den = num[:, ML_V_DIM:ML_V_DIM + 1]
    h = num[:, :ML_V_DIM] / jnp.maximum(jnp.abs(den), jnp.exp(-m_out))
    return h, c_new, m_new


def _mlstm_kernel(par_ref, qkf_ref, vf_ref, qkb_ref, vb_ref, gf_ref, gb_ref, of_ref, ob_ref,
                  cf_ref, cb_ref, mf_ref, mb_ref):
    h = pl.program_id(1)
    i = pl.program_id(2)
    nch = qkf_ref.shape[1] // CHUNK

    @pl.when(i == 0)
    def _():
        for ref in (cf_ref, cb_ref, mf_ref, mb_ref):
            ref[...] = jnp.zeros_like(ref)

    ones_col = jnp.where(_iota2((CHUNK, ML_V_DIM), 1) == 0, 1.0, 0.0).astype(BF16)

    def one(qk_ref, v_ref, g_ref, o_ref, c_ref, m_ref, c, ch, rev):
        r0 = pl.multiple_of(c * CHUNK, CHUNK)
        rows = pl.ds(r0, CHUNK)
        i_row = g_ref[0, c, pl.ds(4 * DN_HEADS + ch, 1), :] + par_ref[4 * DN_HEADS + ch]
        f_row = g_ref[0, c, pl.ds(4 * DN_HEADS + 2 * ML_HEADS + ch, 1), :] + par_ref[4 * DN_HEADS + 2 * ML_HEADS + ch]
        qk = qk_ref[0, rows, :]
        q = qk[:, :ML_QK_DIM]
        k = (qk[:, ML_QK_DIM:].astype(F32) * ML_QK_DIM ** -0.5).astype(BF16)
        v_ext = jnp.concatenate([v_ref[0, rows, :], ones_col], axis=1)
        hh, c_new, m_new = _mlstm_chunk(q, k, v_ext, i_row, f_row, c_ref[...], m_ref[...], rev)
        o_ref[0, rows, :] = hh
        c_ref[...] = c_new
        m_ref[...] = m_new

    def body(c, carry):
        one(qkf_ref, vf_ref, gf_ref, of_ref, cf_ref, mf_ref, c, h, False)
        one(qkb_ref, vb_ref, gb_ref, ob_ref, cb_ref, mb_ref, nch - 1 - c, ML_HEADS + h, True)
        return carry

    lax.fori_loop(0, nch, body, 0)


def _mlstm(params, proj3, gates_t):
    nb, ns, _ = proj3.shape
    ts = SEQ_BLOCK
    nblk = ns // ts
    nch = ts // CHUNK
    qk_off = 5 * MIX_WIDTH // LANES
    v_off = 6 * MIX_WIDTH // LANES

    def col(off, reverse):
        if reverse:
            return lambda b, h, i: (b, nblk - 1 - i, off + h)
        return lambda b, h, i: (b, i, off + h)

    blk = lambda off, reverse: pl.BlockSpec((1, ts, LANES), col(off, reverse))
    gspec = lambda reverse: pl.BlockSpec(
        (1, nch, 32, CHUNK), (lambda b, h, i: (b, nblk - 1 - i, 0, 0)) if reverse else (lambda b, h, i: (b, i, 0, 0)))
    out_sd = jax.ShapeDtypeStruct((nb, ns, MIX_WIDTH), F32)
    return pl.pallas_call(
        _mlstm_kernel, grid=(nb, ML_HEADS, nblk),
        in_specs=[pl.BlockSpec(memory_space=pltpu.SMEM),
                  blk(qk_off, False), blk(v_off, False), blk(qk_off, True), blk(v_off, True),
                  gspec(False), gspec(True)],
        out_specs=[pl.BlockSpec((1, ts, LANES), col(0, False)), pl.BlockSpec((1, ts, LANES), col(0, True))],
        out_shape=[out_sd, out_sd],
        scratch_shapes=[pltpu.VMEM((ML_QK_DIM, 2 * ML_V_DIM), F32), pltpu.VMEM((ML_QK_DIM, 2 * ML_V_DIM), F32),
                        pltpu.VMEM((1, LANES), F32), pltpu.VMEM((1, LANES), F32)],
        compiler_params=_cparams(("parallel", "parallel", "arbitrary")), name="mlstm",
    )(params, proj3, proj3, proj3, proj3, gates_t, gates_t)


def _pool_kernel(u_ref, up_ref, un_ref, w_ref, sc_ref, o_ref, xs_ref):
    i = pl.program_id(1)
    nblk = pl.num_programs(1)
    ts = u_ref.shape[1]
    _fill_halo(xs_ref, u_ref[0].astype(F32), up_ref[0].astype(F32), un_ref[0].astype(F32), i, nblk)
    pos = i * ts + _iota2((ts, 1), 0)
    seq = nblk * ts
    for g, win in enumerate(POOL_WINDOWS):
        half = win // 2
        cols = slice(g * POOL_DIM, (g + 1) * POOL_DIM)
        acc = jnp.zeros((ts, POOL_DIM), F32)
        for d in range(-half, half):
            acc = acc + xs_ref[8 + d:8 + d + ts, cols]
        cnt = (jnp.minimum(pos + half, seq) - jnp.maximum(pos - half, 0)).astype(F32)
        p = acc / cnt - xs_ref[8:8 + ts, cols]
        y = _dot(p.astype(BF16), w_ref[g]) * sc_ref[:, cols]
        o_ref[0, :, cols] = y.astype(BF16)


def _pool(proj3, pool_w, pool_scale):
    nb, ns, _ = proj3.shape
    ts = SEQ_BLOCK
    cblk = 4
    main, prev, make_next = _halo_specs(ts, MIX_WIDTH, lambda: cblk)
    nxt = make_next(ns // HALO)
    return pl.pallas_call(
        _pool_kernel, grid=(nb, ns // ts),
        in_specs=[pl.BlockSpec((1, ts, MIX_WIDTH), main), pl.BlockSpec((1, HALO, MIX_WIDTH), prev),
                  pl.BlockSpec((1, HALO, MIX_WIDTH), nxt),
                  pl.BlockSpec((len(POOL_WINDOWS), POOL_DIM, POOL_DIM), lambda b, i: (0, 0, 0)),
                  pl.BlockSpec((1, MIX_WIDTH), lambda b, i: (0, 0))],
        out_specs=pl.BlockSpec((1, ts, MIX_WIDTH), lambda b, i: (b, i, 0)),
        out_shape=jax.ShapeDtypeStruct((nb, ns, MIX_WIDTH), BF16),
        scratch_shapes=[pltpu.VMEM((ts + 16, MIX_WIDTH), F32)],
        compiler_params=_cparams(("parallel", "parallel")), name="pool",
    )(proj3, proj3, proj3, pool_w, pool_scale)


def _merge_kernel(x_ref, h_ref, of_ref, ob_ref, z_ref, yp_ref, mf_ref, mb_ref, mo_ref,
                  wg_ref, wb_ref, wo_ref, dnw_ref, mlw_ref, lg_ref, lb_ref, o_ref, obf_ref):
    o_dn = of_ref[...] + ob_ref[...]
    h_ml = mf_ref[...] + mb_ref[...]
    dn_parts = []
    ml_parts = []
    for hd in range(DN_HEADS):
        cols = slice(hd * DN_DIM, (hd + 1) * DN_DIM)
        a = o_dn[:, cols]
        dn_parts.append(a * lax.rsqrt(jnp.mean(a * a, -1, keepdims=True) + NORM_EPS))
        c = h_ml[:, cols]
        c = c - jnp.mean(c, -1, keepdims=True)
        ml_parts.append(c * lax.rsqrt(jnp.mean(c * c, -1, keepdims=True) + NORM_EPS))
    z = z_ref[...].astype(F32)
    y_dn = jnp.concatenate(dn_parts, axis=1) * dnw_ref[...] * (z * _sigmoid(z))
    y_ml = jnp.concatenate(ml_parts, axis=1) * mlw_ref[...] * _sigmoid(mo_ref[...].astype(F32))
    branches = (y_dn.astype(BF16), yp_ref[...], y_ml.astype(BF16))
    x = x_ref[...]
    merged = jnp.zeros(o_ref.shape, F32)
    for r in range(N_BRANCH):
        gate = _sigmoid(_dot(x, wg_ref[:, r * D_MODEL:(r + 1) * D_MODEL]))
        merged = merged + gate * _dot(branches[r], wb_ref[r])
    mix = _dot(merged.astype(BF16), wo_ref[...])
    y = _ln_rows(ALPHA * h_ref[...] + mix, lg_ref[...], lb_ref[...])
    o_ref[...] = y
    obf_ref[...] = y.astype(BF16)


def _merge(x_bf, h, o_f, o_b, proj, y_pool, m_f, m_b, w_gate, w_branch, w_out, dn_norm_w, ml_norm_w, ln_g, ln_b):
    t, d = h.shape
    tm = ROW_BLOCK
    row = lambda width, cblk=0: pl.BlockSpec((tm, width), lambda i: (i, cblk))
    const = lambda shape: pl.BlockSpec(shape, lambda i: (0,) * len(shape), pipeline_mode=pl.Buffered(1))
    return pl.pallas_call(
        _merge_kernel, grid=(t // tm,),
        in_specs=[row(d), row(d), row(MIX_WIDTH), row(MIX_WIDTH), row(MIX_WIDTH, 3), row(MIX_WIDTH),
                  row(MIX_WIDTH), row(MIX_WIDTH), row(MIX_WIDTH, 7),
                  const((d, N_BRANCH * d)), const((N_BRANCH, MIX_WIDTH, d)), const((d, d)),
                  const((1, MIX_WIDTH)), const((1, MIX_WIDTH)), const((1, d)), const((1, d))],
        out_specs=[row(d), row(d)],
        out_shape=[jax.ShapeDtypeStruct((t, d), F32), jax.ShapeDtypeStruct((t, d), BF16)],
        compiler_params=_cparams(("parallel",)), name="merge",
    )(x_bf, h, o_f, o_b, proj, y_pool, m_f, m_b, proj, w_gate, w_branch, w_out,
      dn_norm_w.reshape(1, -1), ml_norm_w.reshape(1, -1), ln_g.reshape(1, -1), ln_b.reshape(1, -1))


def _route_kernel(h_ref, rw_ref, rb_ref, e_ref, gw_ref, rank_ref, cnt_ref, base_ref):
    i = pl.program_id(0)
    tm = h_ref.shape[0]
    G, J = N_GROUPS, EXPERTS_PER_GROUP

    @pl.when(i == 0)
    def _():
        base_ref[...] = jnp.zeros_like(base_ref)

    pre = jnp.dot(h_ref[...], rw_ref[...], preferred_element_type=F32, precision=lax.Precision.HIGHEST)
    sel_rows = jnp.where(_iota2((N_EXPERTS, LANES), 0) == _iota2((N_EXPERTS, LANES), 1), 1.0, 0.0).astype(BF16)
    p1, p2, p3 = _split3(pre)
    pre_t = _dot_nt(sel_rows, p1) + (_dot_nt(sel_rows, p2) + _dot_nt(sel_rows, p3))
    score = [_sigmoid(pre_t[j * G:(j + 1) * G, :]) for j in range(J)]
    sel = [score[j] + rb_ref[j][:, :1] for j in range(J)]
    pair = None
    for a in range(J):
        for b in range(a + 1, J):
            s = sel[a] + sel[b]
            pair = s if pair is None else jnp.maximum(pair, s)
    gid = _iota2((G, tm), 0)
    gmax = jnp.max(pair, axis=0, keepdims=True)
    grp = jnp.min(jnp.where(pair == gmax, gid.astype(F32), float(G)), axis=0, keepdims=True).astype(jnp.int32)
    in_grp = gid == grp
    pick = lambda arr: jnp.sum(jnp.where(in_grp, arr, 0.0), axis=0, keepdims=True)
    v = [pick(sel[j]) for j in range(J)]
    u = [pick(score[j]) for j in range(J)]
    rank = []
    for j in range(J):
        rj = jnp.zeros((1, tm), jnp.int32)
        for a in range(J):
            if a == j:
                continue
            ahead = (v[a] >= v[j]) if a < j else (v[a] > v[j])
            rj = rj + ahead.astype(jnp.int32)
        rank.append(rj)
    loc = []
    wsel = []
    for kk in range(TOP_K):
        lk = jnp.zeros((1, tm), jnp.int32)
        wk = jnp.zeros((1, tm), F32)
        for j in range(J):
            hit = rank[j] == kk
            lk = lk + jnp.where(hit, j, 0)
            wk = wk + jnp.where(hit, u[j], 0.0)
        loc.append(lk)
        wsel.append(wk)
    wsum = wsel[0] + wsel[1]
    e_ref[...] = jnp.concatenate([grp * J + loc[0], grp * J + loc[1]], axis=0)
    gw_ref[...] = jnp.concatenate([wsel[0] / wsum, wsel[1] / wsum], axis=0)
    before = jnp.where(_iota2((tm, tm), 0) < _iota2((tm, tm), 1), 1.0, 0.0).astype(BF16)
    ones = jnp.ones((tm, LANES), BF16)
    r_k = [jnp.zeros((1, tm), F32) for _ in range(TOP_K)]
    for j in range(J):
        hits = [in_grp & (loc[kk] == j) for kk in range(TOP_K)]
        oh = jnp.where(hits[0] | hits[1], 1.0, 0.0)
        pos = _dot(oh.astype(BF16), before) + base_ref[j][:, :1]
        for kk in range(TOP_K):
            r_k[kk] = r_k[kk] + jnp.sum(jnp.where(hits[kk], pos, 0.0), axis=0, keepdims=True)
        base_ref[j] = base_ref[j] + _dot(oh.astype(BF16), ones)
    rank_ref[...] = jnp.concatenate(r_k, axis=0).astype(jnp.int32)
    cnt_ref[...] = base_ref[...]


def _route(h, router_w, router_b):
    t, d = h.shape
    tm = SEQ_BLOCK
    G, J = N_GROUPS, EXPERTS_PER_GROUP
    perm = np.asarray([g * J + j for j in range(J) for g in range(G)])
    rw = jnp.zeros((d, LANES), F32).at[:, :N_EXPERTS].set(router_w.astype(F32)[:, perm])
    rb = jnp.broadcast_to(router_b.astype(F32)[perm].reshape(J, G, 1), (J, G, LANES))
    pair = lambda dt: jax.ShapeDtypeStruct((TOP_K, t), dt)
    tok = pl.BlockSpec((TOP_K, tm), lambda i: (0, i))
    return pl.pallas_call(
        _route_kernel, grid=(t // tm,),
        in_specs=[pl.BlockSpec((tm, d), lambda i: (i, 0)), pl.BlockSpec((d, LANES), lambda i: (0, 0)),
                  pl.BlockSpec((J, G, LANES), lambda i: (0, 0, 0))],
        out_specs=[tok, tok, tok, pl.BlockSpec((J, G, LANES), lambda i: (0, 0, 0))],
        out_shape=[pair(jnp.int32), pair(F32), pair(jnp.int32), jax.ShapeDtypeStruct((J, G, LANES), F32)],
        scratch_shapes=[pltpu.VMEM((J, G, LANES), F32)],
        compiler_params=_cparams(("arbitrary",)), name="route",
    )(h, rw, rb)


def _dispatch_kernel(dest_ref, h_ref, xb_in_ref, xb_ref, sem):
    del xb_in_ref
    tm = h_ref.shape[0]

    def row_copy(t, kk):
        return pltpu.make_async_copy(h_ref.at[pl.ds(t, 1)], xb_ref.at[pl.ds(dest_ref[kk, t], 1)], sem)

    def start(t, carry):
        for kk in range(TOP_K):
            row_copy(t, kk).start()
        return carry

    def wait(t, carry):
        for kk in range(TOP_K):
            row_copy(t, kk).wait()
        return carry

    lax.fori_loop(0, tm, start, 0)
    lax.fori_loop(0, tm, wait, 0)


def _dispatch(dest, h, n_rows):
    t, d = h.shape
    tm = ROW_BLOCK
    xb0 = jnp.zeros((n_rows, d), F32)
    return pl.pallas_call(
        _dispatch_kernel, grid=(t // tm,),
        in_specs=[pl.BlockSpec((TOP_K, tm), lambda i: (0, i), memory_space=pltpu.SMEM),
                  pl.BlockSpec((tm, d), lambda i: (i, 0)), pl.BlockSpec(memory_space=pl.ANY)],
        out_specs=pl.BlockSpec(memory_space=pl.ANY),
        out_shape=jax.ShapeDtypeStruct((n_rows, d), F32),
        scratch_shapes=[pltpu.SemaphoreType.DMA(())],
        input_output_aliases={2: 0},
        compiler_params=_cparams(("arbitrary",)), name="moe_dispatch",
    )(dest, h, xb0)


def _expert_kernel(be_ref, nact_ref, x_ref, w1_ref, w3_ref, w2_ref, o_ref):
    i = pl.program_id(0)

    @pl.when(i < nact_ref[0])
    def _():
        x = x_ref[...].astype(BF16)
        a = _dot(x, w1_ref[0])
        b = _dot(x, w3_ref[0])
        act = a * _sigmoid(a) * b
        o_ref[...] = _dot(act.astype(BF16), w2_ref[0])

    @pl.when(i >= nact_ref[0])
    def _():
        o_ref[...] = jnp.zeros_like(o_ref)


def _experts(block_expert, n_active, xb, w1, w3, w2):
    n_rows, d = xb.shape
    bm = MOE_ROWS
    grid_spec = pltpu.PrefetchScalarGridSpec(
        num_scalar_prefetch=2, grid=(n_rows // bm,),
        in_specs=[pl.BlockSpec((bm, d), lambda i, be, na: (i, 0)),
                  pl.BlockSpec((1, d, D_EXPERT), lambda i, be, na: (be[i], 0, 0)),
                  pl.BlockSpec((1, d, D_EXPERT), lambda i, be, na: (be[i], 0, 0)),
                  pl.BlockSpec((1, D_EXPERT, d), lambda i, be, na: (be[i], 0, 0))],
        out_specs=pl.BlockSpec((bm, d), lambda i, be, na: (i, 0)))
    return pl.pallas_call(
        _expert_kernel, grid_spec=grid_spec, out_shape=jax.ShapeDtypeStruct((n_rows, d), F32),
        compiler_params=_cparams(("arbitrary",)), name="moe_experts",
    )(block_expert, n_active, xb, w1, w3, w2)


def _combine_kernel(dest_ref, h_ref, gw_ref, yb_ref, lg_ref, lb_ref, o_ref, obf_ref, buf_ref, sem):
    tm = h_ref.shape[0]

    def row_copy(t, kk):
        return pltpu.make_async_copy(yb_ref.at[pl.ds(dest_ref[kk, t], 1)], buf_ref.at[kk, pl.ds(t, 1)], sem)

    def start(t, carry):
        for kk in range(TOP_K):
            row_copy(t, kk).start()
        return carry

    def wait(t, carry):
        for kk in range(TOP_K):
            row_copy(t, kk).wait()
        return carry

    lax.fori_loop(0, tm, start, 0)
    lax.fori_loop(0, tm, wait, 0)
    gw = gw_ref[...]
    ffn = gw[:, 0:1] * buf_ref[0] + gw[:, 1:2] * buf_ref[1]
    y = _ln_rows(ALPHA * h_ref[...] + ffn, lg_ref[...], lb_ref[...])
    o_ref[...] = y
    obf_ref[...] = y.astype(BF16)


def _combine(dest, h, gw_cols, yb, ln_g, ln_b):
    t, d = h.shape
    tm = ROW_BLOCK
    row = pl.BlockSpec((tm, d), lambda i: (i, 0))
    vec = pl.BlockSpec((1, d), lambda i: (0, 0))
    return pl.pallas_call(
        _combine_kernel, grid=(t // tm,),
        in_specs=[pl.BlockSpec((TOP_K, tm), lambda i: (0, i), memory_space=pltpu.SMEM), row,
                  pl.BlockSpec((tm, TOP_K), lambda i: (i, 0)), pl.BlockSpec(memory_space=pl.ANY), vec, vec],
        out_specs=[row, row],
        out_shape=[jax.ShapeDtypeStruct((t, d), F32), jax.ShapeDtypeStruct((t, d), BF16)],
        scratch_shapes=[pltpu.VMEM((TOP_K, tm, d), F32), pltpu.SemaphoreType.DMA(())],
        compiler_params=_cparams(("arbitrary",)), name="moe_combine",
    )(dest, h, gw_cols, yb, ln_g.reshape(1, d), ln_b.reshape(1, d))


def _moe(h, router_w, router_b, w1, w3, w2, ln_g, ln_b):
    t, d = h.shape
    expert, gate_w, rank, counts = _route(h, router_w, router_b)
    cnt = counts[:, :, 0].T.reshape(N_EXPERTS).astype(jnp.int32)
    padded = ((cnt + MOE_ROWS - 1) // MOE_ROWS) * MOE_ROWS
    pend = jnp.cumsum(padded)
    pstart = pend - padded
    dest = pstart[expert] + rank
    n_rows = t * TOP_K + N_EXPERTS * MOE_ROWS
    n_blocks = n_rows // MOE_ROWS
    block_expert = jnp.clip(jnp.searchsorted(pend, jnp.arange(n_blocks, dtype=jnp.int32) * MOE_ROWS, side='right'),
                            0, N_EXPERTS - 1).astype(jnp.int32)
    n_active = (pend[-1:] // MOE_ROWS).astype(jnp.int32)
    xb = _dispatch(dest, h, n_rows)
    yb = _experts(block_expert, n_active, xb, w1, w3, w2)
    return _combine(dest, h, gate_w.T, yb, ln_g, ln_b)


def kernel(x, ln_in_g, ln_in_b, w_in, dn_conv_w, dn_a_log, dn_dt_bias, dn_norm_w, pool_w, pool_scale,
           ml_gate_b, ml_norm_w, w_branch, w_out, ln_mix_g, ln_mix_b, router_w, router_b,
           moe_w1, moe_w3, moe_w2, ln_ffn_g, ln_ffn_b):
    nb, ns, d = x.shape
    t = nb * ns
    main_cols, small_cols, gate_cols = _column_orders()
    h, h_bf = _layer_norm_in(x.reshape(t, d), ln_in_g, ln_in_b)
    for l in range(DEPTH):
        w_l = w_in[l]
        w_main = w_l[:, main_cols].astype(BF16)
        w_small = jnp.zeros((d, LANES), BF16).at[:, :len(small_cols)].set(w_l[:, small_cols].astype(BF16))
        w_gate = w_l[:, gate_cols].astype(BF16)
        proj = _matmul(h_bf, w_main, BF16, 512, 1024, "in_proj")
        small = _matmul(h_bf, w_small, F32, 512, LANES, "in_proj_gates")
        proj3 = proj.reshape(nb, ns, MAIN_WIDTH)
        gates_t = small[:, :32].reshape(nb, ns // CHUNK, CHUNK, 32).transpose(0, 1, 3, 2)
        params = jnp.concatenate([dn_a_log[l], dn_dt_bias[l], ml_gate_b[l]]).astype(F32)
        qkv = _dn_prep(proj3, dn_conv_w[l].astype(F32))
        o_f, o_b = _gdn(params, qkv, gates_t)
        y_pool = _pool(proj3, pool_w[l].astype(BF16), pool_scale[l].astype(F32).reshape(1, -1))
        m_f, m_b = _mlstm(params, proj3, gates_t)
        flat = lambda a: a.reshape(t, -1)
        h, h_bf = _merge(h_bf, h, flat(o_f), flat(o_b), proj, flat(y_pool), flat(m_f), flat(m_b),
                         w_gate, w_branch[l].astype(BF16), w_out[l].astype(BF16),
                         dn_norm_w[l].astype(F32), ml_norm_w[l].astype(F32), ln_mix_g[l], ln_mix_b[l])
        h, h_bf = _moe(h, router_w, router_b, moe_w1[l].astype(BF16), moe_w3[l].astype(BF16),
                       moe_w2[l].astype(BF16), ln_ffn_g[l], ln_ffn_b[l])
    return h.reshape(nb, ns, d)
```

```python
import jax
import jax.numpy as jnp
import numpy as np
from jax import lax
from jax.experimental import pallas as pl
from jax.experimental.pallas import tpu as pltpu

F32 = jnp.float32
BF16 = jnp.bfloat16

D_MODEL = 1024
DEPTH = 2
MIX_WIDTH = D_MODEL // 2
DN_HEADS = 4
DN_DIM = MIX_WIDTH // DN_HEADS
DN_CONV = 5
CHUNK = 64
POOL_WINDOWS = (2, 4, 8, 16)
POOL_DIM = MIX_WIDTH // len(POOL_WINDOWS)
ML_HEADS = 4
ML_V_DIM = MIX_WIDTH // ML_HEADS
ML_QK_DIM = ML_V_DIM // 2
N_BRANCH = 3
N_EXPERTS = 32
N_GROUPS = 8
EXPERTS_PER_GROUP = N_EXPERTS // N_GROUPS
TOP_K = 2
D_EXPERT = D_MODEL // 2
LN_EPS = 1e-5
NORM_EPS = 1e-6
ALPHA = (2.0 * DEPTH) ** 0.25

LANES = 128
HALO = 16
SEQ_BLOCK = 512
ROW_BLOCK = 256
MOE_ROWS = 256
DISPATCH_BLOCK = 256
COMBINE_BLOCK = 128
NEG_BIG = -1e30
VMEM_LIMIT = 56 * 1024 * 1024

_SPLITS = (3 * MIX_WIDTH, MIX_WIDTH, 2 * DN_HEADS, 2 * DN_HEADS, MIX_WIDTH, ML_HEADS * ML_QK_DIM,
           ML_HEADS * ML_QK_DIM, MIX_WIDTH, MIX_WIDTH, 2 * ML_HEADS, 2 * ML_HEADS, N_BRANCH * D_MODEL)
_OFF = [0] + [int(c) for c in np.cumsum(_SPLITS)]
MAIN_WIDTH = 8 * MIX_WIDTH


def _split_in_proj(w):
    o = _OFF
    w = w.astype(BF16)
    parts = [w[:, o[0]:o[2]], w[:, o[4]:o[5]]]
    for h in range(ML_HEADS):
        parts.append(w[:, o[5] + h * ML_QK_DIM:o[5] + (h + 1) * ML_QK_DIM])
        parts.append(w[:, o[6] + h * ML_QK_DIM:o[6] + (h + 1) * ML_QK_DIM])
    parts.append(w[:, o[7]:o[9]])
    main = jnp.concatenate(parts, axis=1)
    n_small = (o[4] - o[2]) + (o[11] - o[9])
    small = jnp.concatenate([w[:, o[2]:o[4]], w[:, o[9]:o[11]], jnp.zeros((w.shape[0], LANES - n_small), BF16)], axis=1)
    return main, small, w[:, o[11]:o[12]]


def _cparams(sem, vmem=VMEM_LIMIT):
    return pltpu.CompilerParams(dimension_semantics=sem, vmem_limit_bytes=vmem)


def _dot(a, b):
    return jnp.dot(a, b, preferred_element_type=F32)


def _dot_nt(a, b):
    return lax.dot_general(a, b, (((1,), (1,)), ((), ())), preferred_element_type=F32)


def _dot_tn(a, b):
    return lax.dot_general(a, b, (((0,), (0,)), ((), ())), preferred_element_type=F32)


def _split3(a):
    a1 = a.astype(BF16)
    r1 = a - a1.astype(F32)
    a2 = r1.astype(BF16)
    a3 = (r1 - a2.astype(F32)).astype(BF16)
    return a1, a2, a3


def _dot_exact_lhs(a, b01):
    a1, a2, a3 = _split3(a)
    return _dot(a1, b01) + (_dot(a2, b01) + _dot(a3, b01))


def _sigmoid(x):
    return 1.0 / (1.0 + jnp.exp(-x))


def _softplus(x):
    return jnp.maximum(x, 0.0) + jnp.log(1.0 + jnp.exp(-jnp.abs(x)))


def _iota2(shape, dim):
    return lax.broadcasted_iota(jnp.int32, shape, dim)


def _ln_rows(x, g, b):
    mu = jnp.mean(x, -1, keepdims=True)
    xc = x - mu
    var = jnp.mean(xc * xc, -1, keepdims=True)
    return xc * lax.rsqrt(var + LN_EPS) * g + b


def _ln_kernel(x_ref, g_ref, b_ref, o_ref, obf_ref):
    y = _ln_rows(x_ref[...], g_ref[...], b_ref[...])
    o_ref[...] = y
    obf_ref[...] = y.astype(BF16)


def _layer_norm_in(x2, g, b):
    t, d = x2.shape
    tm = ROW_BLOCK
    row = pl.BlockSpec((tm, d), lambda i: (i, 0))
    vec = pl.BlockSpec((1, d), lambda i: (0, 0))
    return pl.pallas_call(
        _ln_kernel, grid=(t // tm,), in_specs=[row, vec, vec], out_specs=[row, row],
        out_shape=[jax.ShapeDtypeStruct((t, d), F32), jax.ShapeDtypeStruct((t, d), BF16)],
        compiler_params=_cparams(("parallel",)), name="ln_in",
    )(x2, g.reshape(1, d), b.reshape(1, d))


def _matmul_kernel(x_ref, w_ref, o_ref):
    o_ref[...] = _dot(x_ref[...], w_ref[...]).astype(o_ref.dtype)


def _matmul(x, w, out_dtype, tm, tn, name):
    t, k = x.shape
    n = w.shape[1]
    return pl.pallas_call(
        _matmul_kernel, grid=(n // tn, t // tm),
        in_specs=[pl.BlockSpec((tm, k), lambda j, i: (i, 0)), pl.BlockSpec((k, tn), lambda j, i: (0, j))],
        out_specs=pl.BlockSpec((tm, tn), lambda j, i: (i, j)),
        out_shape=jax.ShapeDtypeStruct((t, n), out_dtype),
        compiler_params=_cparams(("parallel", "parallel")), name=name,
    )(x, w)


def _fill_halo(xs_ref, main, prev, nxt, i, nblk):
    ts = main.shape[0]
    xs_ref[0:8, :] = jnp.where(i > 0, prev[HALO - 8:HALO, :], 0.0)
    xs_ref[8:8 + ts, :] = main
    xs_ref[8 + ts:16 + ts, :] = jnp.where(i < nblk - 1, nxt[0:8, :], 0.0)


def _dn_prep_kernel(x_ref, xp_ref, xn_ref, cw_ref, o_ref, xs_ref):
    i = pl.program_id(1)
    j = pl.program_id(2)
    ts = x_ref.shape[1]
    _fill_halo(xs_ref, x_ref[0].astype(F32), xp_ref[0].astype(F32), xn_ref[0].astype(F32), i, pl.num_programs(1))
    pad = DN_CONV // 2
    for hd in range(DN_HEADS):
        cols = slice(hd * DN_DIM, (hd + 1) * DN_DIM)
        acc = jnp.zeros((ts, DN_DIM), F32)
        for w in range(DN_CONV):
            acc = acc + xs_ref[8 + w - pad:8 + w - pad + ts, cols] * cw_ref[w:w + 1, cols]
        y = acc * _sigmoid(acc)
        inv = lax.rsqrt(jnp.sum(y * y, -1, keepdims=True) + NORM_EPS)
        scale = jnp.where(j == 0, inv * DN_DIM ** -0.5, jnp.where(j == 1, inv, 1.0))
        o_ref[0, :, cols] = (y * scale).astype(BF16)


def _halo_specs(ts, width, col_of):
    r = ts // HALO

    def main(b, i, *rest):
        return (b, i, col_of(*rest))

    def prev(b, i, *rest):
        return (b, jnp.maximum(i * r - 1, 0), col_of(*rest))

    def make_next(nrow_blocks):
        def nxt(b, i, *rest):
            return (b, jnp.minimum((i + 1) * r, nrow_blocks - 1), col_of(*rest))
        return nxt

    return main, prev, make_next


def _dn_prep(proj3, conv_w):
    nb, ns, _ = proj3.shape
    ts = SEQ_BLOCK
    main, prev, make_next = _halo_specs(ts, MIX_WIDTH, lambda j: j)
    nxt = make_next(ns // HALO)
    return pl.pallas_call(
        _dn_prep_kernel, grid=(nb, ns // ts, 3),
        in_specs=[pl.BlockSpec((1, ts, MIX_WIDTH), main), pl.BlockSpec((1, HALO, MIX_WIDTH), prev),
                  pl.BlockSpec((1, HALO, MIX_WIDTH), nxt), pl.BlockSpec((DN_CONV, MIX_WIDTH), lambda b, i, j: (0, j))],
        out_specs=pl.BlockSpec((1, ts, MIX_WIDTH), main),
        out_shape=jax.ShapeDtypeStruct((nb, ns, 3 * MIX_WIDTH), BF16),
        scratch_shapes=[pltpu.VMEM((ts + 16, MIX_WIDTH), F32)],
        compiler_params=_cparams(("parallel", "parallel", "parallel")), name="dn_prep",
    )(proj3, proj3, proj3, conv_w)


def _chunk_masks(rev):
    r = _iota2((CHUNK, CHUNK), 0)
    c = _iota2((CHUNK, CHUNK), 1)
    if rev:
        return r <= c, r < c, r >= c
    return r >= c, r > c, r <= c


def _col_bcast2(row_a, row_b):
    r = _iota2((CHUNK, CHUNK), 0)
    c = _iota2((CHUNK, CHUNK), 1)
    eye = r == c
    diag = jnp.concatenate([jnp.where(eye, jnp.broadcast_to(row_a, (CHUNK, CHUNK)), 0.0),
                            jnp.where(eye, jnp.broadcast_to(row_b, (CHUNK, CHUNK)), 0.0)], axis=1)
    rr = _iota2((2 * CHUNK, 2 * LANES), 0)
    cc = _iota2((2 * CHUNK, 2 * LANES), 1)
    ones_bd = jnp.where((rr // CHUNK) == (cc // LANES), 1.0, 0.0).astype(BF16)
    out = _dot_exact_lhs(diag, ones_bd)
    return out[:, :LANES], out[:, LANES:]


def _bf(x):
    return x.astype(BF16)


def _each(items, fn):
    for it in items:
        fn(it)


def _tri_inverse_offdiag(items):
    r = _iota2((CHUNK, CHUNK), 0)
    c = _iota2((CHUNK, CHUNK), 1)
    same16 = (r // 16) == (c // 16)
    same32 = (r // 32) == (c // 32)

    def start(it):
        it['md'] = jnp.where(same16, it['m'], 0.0)
        md_bf = _bf(it['md'])
        it['n'] = -it['md']
        it['a'] = _dot(md_bf, md_bf)

    def square(it):
        a_bf = _bf(it['a'])
        both = _dot(jnp.concatenate([_bf(it['n']), a_bf], axis=0), a_bf)
        it['n'] = it['n'] + it['a'] + both[:CHUNK]
        it['a'] = both[CHUNK:]

    def last_power(it):
        it['n'] = it['n'] + it['a'] + _dot(_bf(it['n']), _bf(it['a']))

    def eliminate_a(mask_of):
        def fn(it):
            cpl = mask_of(it)
            it['x'] = cpl + _dot(_bf(it['n']), _bf(cpl))
        return fn

    def eliminate_b(it):
        it['n'] = it['n'] - it['x'] - _dot(_bf(it['x']), _bf(it['n']))

    _each(items, start)
    _each(items, square)
    _each(items, square)
    _each(items, last_power)
    _each(items, eliminate_a(lambda it: jnp.where(same32, it['m'] - it['md'], 0.0)))
    _each(items, eliminate_b)
    _each(items, eliminate_a(lambda it: jnp.where(same32, 0.0, it['m'])))
    _each(items, eliminate_b)


def _gdn_kernel(par_ref, qf_ref, kf_ref, vf_ref, qb_ref, kb_ref, vb_ref, gf_ref, gb_ref,
                of_ref, ob_ref, sf_ref, sb_ref):
    h = pl.program_id(1)
    i = pl.program_id(2)
    nch = qf_ref.shape[1] // CHUNK

    @pl.when(i == 0)
    def _():
        sf_ref[...] = jnp.zeros_like(sf_ref)
        sb_ref[...] = jnp.zeros_like(sb_ref)

    items = []
    scans = []
    for q_ref, k_ref, v_ref, g_ref, o_ref, s_ref, ch, rev in (
            (qf_ref, kf_ref, vf_ref, gf_ref, of_ref, sf_ref, h, False),
            (qb_ref, kb_ref, vb_ref, gb_ref, ob_ref, sb_ref, DN_HEADS + h, True)):
        incl, strict, cum = _chunk_masks(rev)
        cum01 = _bf(jnp.where(cum, 1.0, 0.0))
        a_rows = g_ref[0, ch]
        b_rows = g_ref[0, 2 * DN_HEADS + ch]
        g_rows = -jnp.exp(jnp.full((nch, CHUNK), par_ref[ch], F32)) * _softplus(a_rows + par_ref[2 * DN_HEADS + ch])
        beta_rows = _sigmoid(b_rows)
        gc_rows = _dot_exact_lhs(g_rows, cum01)
        g_tots = _dot_exact_lhs(g_rows, jnp.ones((CHUNK, LANES), BF16))
        order = []
        for c in (range(nch - 1, -1, -1) if rev else range(nch)):
            rows = slice(c * CHUNK, (c + 1) * CHUNK)
            it = dict(q_ref=q_ref, k_ref=k_ref, v_ref=v_ref, o_ref=o_ref, rows=rows, incl=incl, strict=strict,
                      gc_row=gc_rows[c:c + 1], beta_row=beta_rows[c:c + 1], g_tot=g_tots[c:c + 1])
            items.append(it)
            order.append(it)
        scans.append((s_ref, order))

    def bcast(it):
        it['gc_col'], it['beta_col'] = _col_bcast2(it['gc_row'], it['beta_row'])

    def gram(it):
        rows = it['rows']
        q = it['q_ref'][0, rows, :]
        k = it['k_ref'][0, rows, :]
        decay = jnp.exp(jnp.where(it['incl'], it['gc_col'][:, :CHUNK] - it['gc_row'], NEG_BIG))
        kb = k.astype(F32) * it['beta_col']
        both = _dot_nt(jnp.concatenate([_bf(kb), q], axis=0), k)
        it['m'] = jnp.where(it['strict'], both[:CHUNK] * decay, 0.0)
        it['attn'] = _bf(both[CHUNK:] * decay)

    def solve(it):
        rows = it['rows']
        q = it['q_ref'][0, rows, :]
        kf = it['k_ref'][0, rows, :].astype(F32)
        v = it['v_ref'][0, rows, :]
        e_col = jnp.exp(it['gc_col'])
        rhs = jnp.concatenate([v.astype(F32) * it['beta_col'], kf * it['beta_col'] * e_col], axis=1)
        sol = rhs + _dot(_bf(it['n']), _bf(rhs))
        it['u'] = sol[:, :DN_DIM]
        it['wq'] = jnp.concatenate([_bf(sol[:, DN_DIM:]), _bf(q.astype(F32) * e_col)], axis=0)
        it['kdec'] = _bf(kf * jnp.exp(it['g_tot'] - it['gc_col']))
        it['e_tot'] = jnp.exp(it['g_tot'])

    _each(items, bcast)
    _each(items, gram)
    _tri_inverse_offdiag(items)
    _each(items, solve)

    states = [s_ref[...] for s_ref, _ in scans]
    for step in range(nch):
        ws = []
        for d, (_, order) in enumerate(scans):
            ws.append(_dot(order[step]['wq'], _bf(states[d])))
        v_new = [_bf(order[step]['u'] - ws[d][:CHUNK]) for d, (_, order) in enumerate(scans)]
        for d, (_, order) in enumerate(scans):
            it = order[step]
            states[d] = states[d] * it['e_tot'] + _dot_tn(it['kdec'], v_new[d])
        for d, (_, order) in enumerate(scans):
            it = order[step]
            it['o_ref'][0, it['rows'], :] = ws[d][CHUNK:] + _dot(it['attn'], v_new[d])
    for d, (s_ref, _) in enumerate(scans):
        s_ref[...] = states[d]


def _gdn(params, qkv, gates_t):
    nb, ns, _ = qkv.shape
    ts = SEQ_BLOCK
    nblk = ns // ts
    nch = ts // CHUNK
    H = DN_HEADS

    def col(off, reverse):
        if reverse:
            return lambda b, h, i: (b, nblk - 1 - i, off + h)
        return lambda b, h, i: (b, i, off + h)

    blk = lambda off, reverse: pl.BlockSpec((1, ts, LANES), col(off, reverse))
    gspec = lambda reverse: pl.BlockSpec(
        (1, 32, nch, CHUNK), (lambda b, h, i: (b, 0, nblk - 1 - i, 0)) if reverse else (lambda b, h, i: (b, 0, i, 0)))
    out_sd = jax.ShapeDtypeStruct((nb, ns, MIX_WIDTH), F32)
    return pl.pallas_call(
        _gdn_kernel, grid=(nb, H, nblk),
        in_specs=[pl.BlockSpec(memory_space=pltpu.SMEM),
                  blk(0, False), blk(H, False), blk(2 * H, False),
                  blk(0, True), blk(H, True), blk(2 * H, True),
                  gspec(False), gspec(True)],
        out_specs=[pl.BlockSpec((1, ts, LANES), col(0, False)), pl.BlockSpec((1, ts, LANES), col(0, True))],
        out_shape=[out_sd, out_sd],
        scratch_shapes=[pltpu.VMEM((DN_DIM, DN_DIM), F32), pltpu.VMEM((DN_DIM, DN_DIM), F32)],
        compiler_params=_cparams(("parallel", "parallel", "arbitrary")), name="gdn",
    )(params, qkv, qkv, qkv, qkv, qkv, qkv, gates_t, gates_t)


def _mlstm_kernel(par_ref, qkf_ref, vf_ref, qkb_ref, vb_ref, gf_ref, gb_ref, of_ref, ob_ref,
                  cf_ref, cb_ref, mf_ref, mb_ref):
    h = pl.program_id(1)
    i = pl.program_id(2)
    nch = qkf_ref.shape[1] // CHUNK

    @pl.when(i == 0)
    def _():
        for ref in (cf_ref, cb_ref, mf_ref, mb_ref):
            ref[...] = jnp.zeros_like(ref)

    ones_col = _bf(jnp.where(_iota2((CHUNK, ML_V_DIM), 1) == 0, 1.0, 0.0))

    items = []
    scans = []
    for qk_ref, v_ref, g_ref, o_ref, c_ref, m_ref, ch, rev in (
            (qkf_ref, vf_ref, gf_ref, of_ref, cf_ref, mf_ref, h, False),
            (qkb_ref, vb_ref, gb_ref, ob_ref, cb_ref, mb_ref, ML_HEADS + h, True)):
        incl, _, cum = _chunk_masks(rev)
        cum01 = _bf(jnp.where(cum, 1.0, 0.0))
        i_rows = g_ref[0, 4 * DN_HEADS + ch] + par_ref[4 * DN_HEADS + ch]
        f_rows = g_ref[0, 4 * DN_HEADS + 2 * ML_HEADS + ch] + par_ref[4 * DN_HEADS + 2 * ML_HEADS + ch]
        logf = -_softplus(-f_rows)
        bc_rows = _dot_exact_lhs(logf, cum01)
        b_tots = _dot_exact_lhs(logf, jnp.ones((CHUNK, LANES), BF16))
        lw_rows = b_tots[:, :CHUNK] - bc_rows + i_rows
        a_maxs = jnp.max(lw_rows, axis=-1, keepdims=True)
        wgt_rows = jnp.exp(lw_rows - a_maxs)
        order = []
        for c in (range(nch - 1, -1, -1) if rev else range(nch)):
            one = slice(c, c + 1)
            it = dict(qk_ref=qk_ref, v_ref=v_ref, o_ref=o_ref, rows=slice(c * CHUNK, (c + 1) * CHUNK), incl=incl,
                      bc_row=bc_rows[one], i_row=i_rows[one], wgt_row=wgt_rows[one], b_tot=b_tots[one],
                      a_max=a_maxs[one])
            items.append(it)
            order.append(it)
        scans.append((c_ref, m_ref, order))

    def load_qkv(it):
        qk = it['qk_ref'][0, it['rows'], :]
        q = qk[:, :ML_QK_DIM]
        kf = qk[:, ML_QK_DIM:].astype(F32) * ML_QK_DIM ** -0.5
        v_ext = jnp.concatenate([it['v_ref'][0, it['rows'], :], ones_col], axis=1)
        return q, kf, v_ext

    def bcast(it):
        it['bc_col'], it['wgt_col'] = _col_bcast2(it['bc_row'], it['wgt_row'])

    def local(it):
        q, kf, v_ext = load_qkv(it)
        it['cc'] = _dot_tn(_bf(kf * it['wgt_col'][:, :ML_QK_DIM]), v_ext)
        it['qk'] = _dot_nt(q, _bf(kf))
        it['d_log'] = jnp.where(it['incl'], it['bc_col'][:, :CHUNK] - it['bc_row'] + it['i_row'], NEG_BIG)
        it['d_max'] = jnp.max(it['d_log'], axis=-1, keepdims=True)

    _each(items, bcast)
    _each(items, local)

    for c_ref, m_ref, order in scans:
        c_state = c_ref[...]
        m_state = m_ref[...]
        for it in order:
            it['c_prev'] = _bf(c_state)
            it['m_prev'] = m_state
            m_new = jnp.maximum(it['b_tot'] + m_state, it['a_max'])
            sp = jnp.exp(it['b_tot'] + m_state - m_new)
            sc = jnp.exp(it['a_max'] - m_new)
            c_state = c_state * jnp.concatenate([sp, sp], axis=1) + it['cc'] * jnp.concatenate([sc, sc], axis=1)
            m_state = m_new
        c_ref[...] = c_state
        m_ref[...] = m_state

    def output(it):
        q, _, v_ext = load_qkv(it)
        inter_log = it['bc_col'] + it['m_prev']
        m_out = jnp.maximum(inter_log, it['d_max'])
        s = it['qk'] * jnp.exp(it['d_log'] - m_out[:, :CHUNK])
        inter = jnp.exp(inter_log - m_out)
        num = jnp.concatenate([inter, inter], axis=1) * _dot(q, it['c_prev']) + _dot(_bf(s), v_ext)
        den = num[:, ML_V_DIM:ML_V_DIM + 1]
        it['o_ref'][0, it['rows'], :] = num[:, :ML_V_DIM] / jnp.maximum(jnp.abs(den), jnp.exp(-m_out))

    _each(items, output)


def _mlstm(params, proj3, gates_t):
    nb, ns, _ = proj3.shape
    ts = SEQ_BLOCK
    nblk = ns // ts
    nch = ts // CHUNK
    qk_off = 5 * MIX_WIDTH // LANES
    v_off = 6 * MIX_WIDTH // LANES

    def col(off, reverse):
        if reverse:
            return lambda b, h, i: (b, nblk - 1 - i, off + h)
        return lambda b, h, i: (b, i, off + h)

    blk = lambda off, reverse: pl.BlockSpec((1, ts, LANES), col(off, reverse))
    gspec = lambda reverse: pl.BlockSpec(
        (1, 32, nch, CHUNK), (lambda b, h, i: (b, 0, nblk - 1 - i, 0)) if reverse else (lambda b, h, i: (b, 0, i, 0)))
    out_sd = jax.ShapeDtypeStruct((nb, ns, MIX_WIDTH), F32)
    return pl.pallas_call(
        _mlstm_kernel, grid=(nb, ML_HEADS, nblk),
        in_specs=[pl.BlockSpec(memory_space=pltpu.SMEM),
                  blk(qk_off, False), blk(v_off, False), blk(qk_off, True), blk(v_off, True),
                  gspec(False), gspec(True)],
        out_specs=[pl.BlockSpec((1, ts, LANES), col(0, False)), pl.BlockSpec((1, ts, LANES), col(0, True))],
        out_shape=[out_sd, out_sd],
        scratch_shapes=[pltpu.VMEM((ML_QK_DIM, 2 * ML_V_DIM), F32), pltpu.VMEM((ML_QK_DIM, 2 * ML_V_DIM), F32),
                        pltpu.VMEM((1, LANES), F32), pltpu.VMEM((1, LANES), F32)],
        compiler_params=_cparams(("parallel", "parallel", "arbitrary")), name="mlstm",
    )(params, proj3, proj3, proj3, proj3, gates_t, gates_t)


def _pool_kernel(u_ref, up_ref, un_ref, w_ref, sc_ref, o_ref, xs_ref):
    i = pl.program_id(1)
    nblk = pl.num_programs(1)
    ts = u_ref.shape[1]
    _fill_halo(xs_ref, u_ref[0].astype(F32), up_ref[0].astype(F32), un_ref[0].astype(F32), i, nblk)
    pos = i * ts + _iota2((ts, 1), 0)
    seq = nblk * ts
    for g, win in enumerate(POOL_WINDOWS):
        half = win // 2
        cols = slice(g * POOL_DIM, (g + 1) * POOL_DIM)
        acc = jnp.zeros((ts, POOL_DIM), F32)
        for d in range(-half, half):
            acc = acc + xs_ref[8 + d:8 + d + ts, cols]
        cnt = (jnp.minimum(pos + half, seq) - jnp.maximum(pos - half, 0)).astype(F32)
        p = acc / cnt - xs_ref[8:8 + ts, cols]
        y = _dot(p.astype(BF16), w_ref[g]) * sc_ref[:, cols]
        o_ref[0, :, cols] = y.astype(BF16)


def _pool(proj3, pool_w, pool_scale):
    nb, ns, _ = proj3.shape
    ts = SEQ_BLOCK
    cblk = 4
    main, prev, make_next = _halo_specs(ts, MIX_WIDTH, lambda: cblk)
    nxt = make_next(ns // HALO)
    return pl.pallas_call(
        _pool_kernel, grid=(nb, ns // ts),
        in_specs=[pl.BlockSpec((1, ts, MIX_WIDTH), main), pl.BlockSpec((1, HALO, MIX_WIDTH), prev),
                  pl.BlockSpec((1, HALO, MIX_WIDTH), nxt),
                  pl.BlockSpec((len(POOL_WINDOWS), POOL_DIM, POOL_DIM), lambda b, i: (0, 0, 0)),
                  pl.BlockSpec((1, MIX_WIDTH), lambda b, i: (0, 0))],
        out_specs=pl.BlockSpec((1, ts, MIX_WIDTH), lambda b, i: (b, i, 0)),
        out_shape=jax.ShapeDtypeStruct((nb, ns, MIX_WIDTH), BF16),
        scratch_shapes=[pltpu.VMEM((ts + 16, MIX_WIDTH), F32)],
        compiler_params=_cparams(("parallel", "parallel")), name="pool",
    )(proj3, proj3, proj3, pool_w, pool_scale)


def _merge_kernel(x_ref, h_ref, of_ref, ob_ref, z_ref, yp_ref, mf_ref, mb_ref, mo_ref,
                  wg_ref, wb_ref, wo_ref, dnw_ref, mlw_ref, lg_ref, lb_ref, o_ref, obf_ref):
    o_dn = of_ref[...] + ob_ref[...]
    h_ml = mf_ref[...] + mb_ref[...]
    dn_parts = []
    ml_parts = []
    for hd in range(DN_HEADS):
        cols = slice(hd * DN_DIM, (hd + 1) * DN_DIM)
        a = o_dn[:, cols]
        dn_parts.append(a * lax.rsqrt(jnp.mean(a * a, -1, keepdims=True) + NORM_EPS))
        c = h_ml[:, cols]
        c = c - jnp.mean(c, -1, keepdims=True)
        ml_parts.append(c * lax.rsqrt(jnp.mean(c * c, -1, keepdims=True) + NORM_EPS))
    z = z_ref[...].astype(F32)
    y_dn = jnp.concatenate(dn_parts, axis=1) * dnw_ref[...] * (z * _sigmoid(z))
    y_ml = jnp.concatenate(ml_parts, axis=1) * mlw_ref[...] * _sigmoid(mo_ref[...].astype(F32))
    branches = (y_dn.astype(BF16), yp_ref[...], y_ml.astype(BF16))
    x = x_ref[...]
    merged = jnp.zeros(o_ref.shape, F32)
    for r in range(N_BRANCH):
        gate = _sigmoid(_dot(x, wg_ref[:, r * D_MODEL:(r + 1) * D_MODEL]))
        merged = merged + gate * _dot(branches[r], wb_ref[r])
    mix = _dot(merged.astype(BF16), wo_ref[...])
    y = _ln_rows(ALPHA * h_ref[...] + mix, lg_ref[...], lb_ref[...])
    o_ref[...] = y
    obf_ref[...] = y.astype(BF16)


def _merge(x_bf, h, o_f, o_b, proj, y_pool, m_f, m_b, w_gate, w_branch, w_out, dn_norm_w, ml_norm_w, ln_g, ln_b):
    t, d = h.shape
    tm = ROW_BLOCK
    row = lambda width, cblk=0: pl.BlockSpec((tm, width), lambda i: (i, cblk))
    const = lambda shape: pl.BlockSpec(shape, lambda i: (0,) * len(shape), pipeline_mode=pl.Buffered(1))
    return pl.pallas_call(
        _merge_kernel, grid=(t // tm,),
        in_specs=[row(d), row(d), row(MIX_WIDTH), row(MIX_WIDTH), row(MIX_WIDTH, 3), row(MIX_WIDTH),
                  row(MIX_WIDTH), row(MIX_WIDTH), row(MIX_WIDTH, 7),
                  const((d, N_BRANCH * d)), const((N_BRANCH, MIX_WIDTH, d)), const((d, d)),
                  const((1, MIX_WIDTH)), const((1, MIX_WIDTH)), const((1, d)), const((1, d))],
        out_specs=[row(d), row(d)],
        out_shape=[jax.ShapeDtypeStruct((t, d), F32), jax.ShapeDtypeStruct((t, d), BF16)],
        compiler_params=_cparams(("parallel",)), name="merge",
    )(x_bf, h, o_f, o_b, proj, y_pool, m_f, m_b, proj, w_gate, w_branch, w_out,
      dn_norm_w.reshape(1, -1), ml_norm_w.reshape(1, -1), ln_g.reshape(1, -1), ln_b.reshape(1, -1))


def _route_kernel(h_ref, rw_ref, rb_ref, e_ref, gw_ref, rank_ref, cnt_ref, base_ref):
    i = pl.program_id(0)
    tm = h_ref.shape[0]
    G, J = N_GROUPS, EXPERTS_PER_GROUP

    @pl.when(i == 0)
    def _():
        base_ref[...] = jnp.zeros_like(base_ref)

    pre = jnp.dot(h_ref[...], rw_ref[...], preferred_element_type=F32, precision=lax.Precision.HIGHEST)
    sel_rows = jnp.where(_iota2((N_EXPERTS, LANES), 0) == _iota2((N_EXPERTS, LANES), 1), 1.0, 0.0).astype(BF16)
    p1, p2, p3 = _split3(pre)
    pre_t = _dot_nt(sel_rows, p1) + (_dot_nt(sel_rows, p2) + _dot_nt(sel_rows, p3))
    score = [_sigmoid(pre_t[j * G:(j + 1) * G, :]) for j in range(J)]
    sel = [score[j] + rb_ref[j][:, :1] for j in range(J)]
    pair = None
    for a in range(J):
        for b in range(a + 1, J):
            s = sel[a] + sel[b]
            pair = s if pair is None else jnp.maximum(pair, s)
    gid = _iota2((G, tm), 0)
    gmax = jnp.max(pair, axis=0, keepdims=True)
    grp = jnp.min(jnp.where(pair == gmax, gid.astype(F32), float(G)), axis=0, keepdims=True).astype(jnp.int32)
    in_grp = gid == grp
    pick = lambda arr: jnp.sum(jnp.where(in_grp, arr, 0.0), axis=0, keepdims=True)
    v = [pick(sel[j]) for j in range(J)]
    u = [pick(score[j]) for j in range(J)]
    rank = []
    for j in range(J):
        rj = jnp.zeros((1, tm), jnp.int32)
        for a in range(J):
            if a == j:
                continue
            ahead = (v[a] >= v[j]) if a < j else (v[a] > v[j])
            rj = rj + ahead.astype(jnp.int32)
        rank.append(rj)
    loc = []
    wsel = []
    for kk in range(TOP_K):
        lk = jnp.zeros((1, tm), jnp.int32)
        wk = jnp.zeros((1, tm), F32)
        for j in range(J):
            hit = rank[j] == kk
            lk = lk + jnp.where(hit, j, 0)
            wk = wk + jnp.where(hit, u[j], 0.0)
        loc.append(lk)
        wsel.append(wk)
    wsum = wsel[0] + wsel[1]
    e_ref[...] = jnp.concatenate([grp * J + loc[0], grp * J + loc[1]], axis=0)
    gw_ref[...] = jnp.concatenate([wsel[0] / wsum, wsel[1] / wsum], axis=0)
    before = jnp.where(_iota2((tm, tm), 0) < _iota2((tm, tm), 1), 1.0, 0.0).astype(BF16)
    ones = jnp.ones((tm, LANES), BF16)
    r_k = [jnp.zeros((1, tm), F32) for _ in range(TOP_K)]
    for j in range(J):
        hits = [in_grp & (loc[kk] == j) for kk in range(TOP_K)]
        oh = jnp.where(hits[0] | hits[1], 1.0, 0.0)
        pos = _dot(oh.astype(BF16), before) + base_ref[j][:, :1]
        for kk in range(TOP_K):
            r_k[kk] = r_k[kk] + jnp.sum(jnp.where(hits[kk], pos, 0.0), axis=0, keepdims=True)
        base_ref[j] = base_ref[j] + _dot(oh.astype(BF16), ones)
    rank_ref[...] = jnp.concatenate(r_k, axis=0).astype(jnp.int32)
    cnt_ref[...] = base_ref[...]


def _route(h, router_w, router_b):
    t, d = h.shape
    tm = SEQ_BLOCK
    G, J = N_GROUPS, EXPERTS_PER_GROUP
    perm = np.asarray([g * J + j for j in range(J) for g in range(G)])
    rw = jnp.zeros((d, LANES), F32).at[:, :N_EXPERTS].set(router_w.astype(F32)[:, perm])
    rb = jnp.broadcast_to(router_b.astype(F32)[perm].reshape(J, G, 1), (J, G, LANES))
    pair = lambda dt: jax.ShapeDtypeStruct((TOP_K, t), dt)
    tok = pl.BlockSpec((TOP_K, tm), lambda i: (0, i))
    return pl.pallas_call(
        _route_kernel, grid=(t // tm,),
        in_specs=[pl.BlockSpec((tm, d), lambda i: (i, 0)), pl.BlockSpec((d, LANES), lambda i: (0, 0)),
                  pl.BlockSpec((J, G, LANES), lambda i: (0, 0, 0))],
        out_specs=[tok, tok, tok, pl.BlockSpec((J, G, LANES), lambda i: (0, 0, 0))],
        out_shape=[pair(jnp.int32), pair(F32), pair(jnp.int32), jax.ShapeDtypeStruct((J, G, LANES), F32)],
        scratch_shapes=[pltpu.VMEM((J, G, LANES), F32)],
        compiler_params=_cparams(("arbitrary",)), name="route",
    )(h, rw, rb)


def _dispatch_kernel(dest_ref, h_ref, xb_in_ref, xb_ref, sem):
    del xb_in_ref
    tm = h_ref.shape[0]

    def row_copy(t, kk):
        return pltpu.make_async_copy(h_ref.at[pl.ds(t, 1)], xb_ref.at[pl.ds(dest_ref[kk, t], 1)], sem)

    def start(t, carry):
        for kk in range(TOP_K):
            row_copy(t, kk).start()
        return carry

    def wait(t, carry):
        for kk in range(TOP_K):
            row_copy(t, kk).wait()
        return carry

    lax.fori_loop(0, tm, start, 0, unroll=8)
    lax.fori_loop(0, tm, wait, 0, unroll=8)


def _dispatch(dest, h, n_rows):
    t, d = h.shape
    tm = DISPATCH_BLOCK
    xb0 = jnp.zeros((n_rows, d), F32)
    return pl.pallas_call(
        _dispatch_kernel, grid=(t // tm,),
        in_specs=[pl.BlockSpec((TOP_K, tm), lambda i: (0, i), memory_space=pltpu.SMEM),
                  pl.BlockSpec((tm, d), lambda i: (i, 0)), pl.BlockSpec(memory_space=pl.ANY)],
        out_specs=pl.BlockSpec(memory_space=pl.ANY),
        out_shape=jax.ShapeDtypeStruct((n_rows, d), F32),
        scratch_shapes=[pltpu.SemaphoreType.DMA(())],
        input_output_aliases={2: 0},
        compiler_params=_cparams(("arbitrary",)), name="moe_dispatch",
    )(dest, h, xb0)


def _expert_kernel(be_ref, nact_ref, x_ref, w1_ref, w3_ref, w2_ref, o_ref):
    i = pl.program_id(0)

    @pl.when(i < nact_ref[0])
    def _():
        x = x_ref[...].astype(BF16)
        a = _dot(x, w1_ref[0])
        b = _dot(x, w3_ref[0])
        act = a * _sigmoid(a) * b
        o_ref[...] = _dot(act.astype(BF16), w2_ref[0])

    @pl.when(i >= nact_ref[0])
    def _():
        o_ref[...] = jnp.zeros_like(o_ref)


def _experts(block_expert, n_active, xb, w1, w3, w2):
    n_rows, d = xb.shape
    bm = MOE_ROWS
    grid_spec = pltpu.PrefetchScalarGridSpec(
        num_scalar_prefetch=2, grid=(n_rows // bm,),
        in_specs=[pl.BlockSpec((bm, d), lambda i, be, na: (i, 0)),
                  pl.BlockSpec((1, d, D_EXPERT), lambda i, be, na: (be[i], 0, 0)),
                  pl.BlockSpec((1, d, D_EXPERT), lambda i, be, na: (be[i], 0, 0)),
                  pl.BlockSpec((1, D_EXPERT, d), lambda i, be, na: (be[i], 0, 0))],
        out_specs=pl.BlockSpec((bm, d), lambda i, be, na: (i, 0)))
    return pl.pallas_call(
        _expert_kernel, grid_spec=grid_spec, out_shape=jax.ShapeDtypeStruct((n_rows, d), F32),
        compiler_params=_cparams(("arbitrary",)), name="moe_experts",
    )(block_expert, n_active, xb, w1, w3, w2)


def _combine_kernel(dest_ref, dest_next_ref, h_ref, gw_ref, yb_ref, lg_ref, lb_ref, o_ref, obf_ref, buf_ref, sem):
    i = pl.program_id(0)
    tm = h_ref.shape[0]
    slot = lax.rem(i, 2)

    def gather(idx_ref, s):
        def start(t, carry):
            for kk in range(TOP_K):
                pltpu.make_async_copy(yb_ref.at[pl.ds(idx_ref[kk, t], 1)], buf_ref.at[s, kk, pl.ds(t, 1)],
                                      sem.at[s]).start()
            return carry
        lax.fori_loop(0, tm, start, 0, unroll=8)

    @pl.when(i == 0)
    def _():
        gather(dest_ref, 0)

    @pl.when(i + 1 < pl.num_programs(0))
    def _():
        gather(dest_next_ref, 1 - slot)

    def wait(t, carry):
        for kk in range(TOP_K):
            pltpu.make_async_copy(yb_ref.at[pl.ds(0, 1)], buf_ref.at[slot, kk, pl.ds(t, 1)], sem.at[slot]).wait()
        return carry

    lax.fori_loop(0, tm, wait, 0, unroll=8)
    gw = gw_ref[...]
    ffn = gw[:, 0:1] * buf_ref[slot, 0] + gw[:, 1:2] * buf_ref[slot, 1]
    y = _ln_rows(ALPHA * h_ref[...] + ffn, lg_ref[...], lb_ref[...])
    o_ref[...] = y
    obf_ref[...] = y.astype(BF16)


def _combine(dest, h, gw_cols, yb, ln_g, ln_b):
    t, d = h.shape
    tm = COMBINE_BLOCK
    row = pl.BlockSpec((tm, d), lambda i: (i, 0))
    vec = pl.BlockSpec((1, d), lambda i: (0, 0))
    nsteps = t // tm
    return pl.pallas_call(
        _combine_kernel, grid=(nsteps,),
        in_specs=[pl.BlockSpec((TOP_K, tm), lambda i: (0, i), memory_space=pltpu.SMEM),
                  pl.BlockSpec((TOP_K, tm), lambda i: (0, jnp.minimum(i + 1, nsteps - 1)), memory_space=pltpu.SMEM),
                  row, pl.BlockSpec((tm, TOP_K), lambda i: (i, 0)), pl.BlockSpec(memory_space=pl.ANY), vec, vec],
        out_specs=[row, row],
        out_shape=[jax.ShapeDtypeStruct((t, d), F32), jax.ShapeDtypeStruct((t, d), BF16)],
        scratch_shapes=[pltpu.VMEM((2, TOP_K, tm, d), F32), pltpu.SemaphoreType.DMA((2,))],
        compiler_params=_cparams(("arbitrary",)), name="moe_combine",
    )(dest, dest, h, gw_cols, yb, ln_g.reshape(1, d), ln_b.reshape(1, d))


def _moe(h, router_w, router_b, w1, w3, w2, ln_g, ln_b):
    t, d = h.shape
    expert, gate_w, rank, counts = _route(h, router_w, router_b)
    cnt = counts[:, :, 0].T.reshape(N_EXPERTS).astype(jnp.int32)
    padded = ((cnt + MOE_ROWS - 1) // MOE_ROWS) * MOE_ROWS
    pend = jnp.cumsum(padded)
    pstart = pend - padded
    dest = pstart[expert] + rank
    n_rows = t * TOP_K + N_EXPERTS * MOE_ROWS
    n_blocks = n_rows // MOE_ROWS
    block_start = jnp.arange(n_blocks, dtype=jnp.int32) * MOE_ROWS
    block_expert = jnp.minimum(jnp.sum((pend[None, :] <= block_start[:, None]).astype(jnp.int32), axis=1), N_EXPERTS - 1)
    n_active = (pend[-1:] // MOE_ROWS).astype(jnp.int32)
    xb = _dispatch(dest, h, n_rows)
    yb = _experts(block_expert, n_active, xb, w1, w3, w2)
    return _combine(dest, h, gate_w.T, yb, ln_g, ln_b)


def kernel(x, ln_in_g, ln_in_b, w_in, dn_conv_w, dn_a_log, dn_dt_bias, dn_norm_w, pool_w, pool_scale,
           ml_gate_b, ml_norm_w, w_branch, w_out, ln_mix_g, ln_mix_b, router_w, router_b,
           moe_w1, moe_w3, moe_w2, ln_ffn_g, ln_ffn_b):
    nb, ns, d = x.shape
    t = nb * ns
    h, h_bf = _layer_norm_in(x.reshape(t, d), ln_in_g, ln_in_b)
    for l in range(DEPTH):
        w_main, w_small, w_gate = _split_in_proj(w_in[l])
        proj = _matmul(h_bf, w_main, BF16, 512, 1024, "in_proj")
        small = _matmul(h_bf, w_small, F32, 512, LANES, "in_proj_gates")
        proj3 = proj.reshape(nb, ns, MAIN_WIDTH)
        gates_t = small[:, :32].reshape(nb, ns // CHUNK, CHUNK, 32).transpose(0, 3, 1, 2)
        params = jnp.concatenate([dn_a_log[l], dn_dt_bias[l], ml_gate_b[l]]).astype(F32)
        qkv = _dn_prep(proj3, dn_conv_w[l].astype(F32))
        o_f, o_b = _gdn(params, qkv, gates_t)
        y_pool = _pool(proj3, pool_w[l].astype(BF16), pool_scale[l].astype(F32).reshape(1, -1))
        m_f, m_b = _mlstm(params, proj3, gates_t)
        flat = lambda a: a.reshape(t, -1)
        h, h_bf = _merge(h_bf, h, flat(o_f), flat(o_b), proj, flat(y_pool), flat(m_f), flat(m_b),
                         w_gate, w_branch[l].astype(BF16), w_out[l].astype(BF16),
                         dn_norm_w[l].astype(F32), ml_norm_w[l].astype(F32), ln_mix_g[l], ln_mix_b[l])
        h, h_bf = _moe(h, router_w, router_b, moe_w1[l].astype(BF16), moe_w3[l].astype(BF16),
                       moe_w2[l].astype(BF16), ln_ffn_g[l], ln_ffn_b[l])
    return h.reshape(nb, ns, d)
```

```python
import jax
import jax.numpy as jnp
import numpy as np
from jax import lax
from jax.experimental import pallas as pl
from jax.experimental.pallas import tpu as pltpu

F32 = jnp.float32
BF16 = jnp.bfloat16

D_MODEL = 1024
DEPTH = 2
MIX_WIDTH = D_MODEL // 2
DN_HEADS = 4
DN_DIM = MIX_WIDTH // DN_HEADS
DN_CONV = 5
CHUNK = 64
POOL_WINDOWS = (2, 4, 8, 16)
POOL_DIM = MIX_WIDTH // len(POOL_WINDOWS)
ML_HEADS = 4
ML_V_DIM = MIX_WIDTH // ML_HEADS
ML_QK_DIM = ML_V_DIM // 2
N_BRANCH = 3
N_EXPERTS = 32
N_GROUPS = 8
EXPERTS_PER_GROUP = N_EXPERTS // N_GROUPS
TOP_K = 2
D_EXPERT = D_MODEL // 2
LN_EPS = 1e-5
NORM_EPS = 1e-6
ALPHA = (2.0 * DEPTH) ** 0.25

LANES = 128
HALO = 16
SEQ_BLOCK = 512
ROW_BLOCK = 256
MOE_ROWS = 256
DISPATCH_BLOCK = 256
COMBINE_BLOCK = 128
NEG_BIG = -1e30
VMEM_LIMIT = 56 * 1024 * 1024

_SPLITS = (3 * MIX_WIDTH, MIX_WIDTH, 2 * DN_HEADS, 2 * DN_HEADS, MIX_WIDTH, ML_HEADS * ML_QK_DIM,
           ML_HEADS * ML_QK_DIM, MIX_WIDTH, MIX_WIDTH, 2 * ML_HEADS, 2 * ML_HEADS, N_BRANCH * D_MODEL)
_OFF = [0] + [int(c) for c in np.cumsum(_SPLITS)]
MAIN_WIDTH = 8 * MIX_WIDTH


def _split_in_proj(w):
    o = _OFF
    w = w.astype(BF16)
    parts = [w[:, o[0]:o[2]], w[:, o[4]:o[5]]]
    for h in range(ML_HEADS):
        parts.append(w[:, o[5] + h * ML_QK_DIM:o[5] + (h + 1) * ML_QK_DIM])
        parts.append(w[:, o[6] + h * ML_QK_DIM:o[6] + (h + 1) * ML_QK_DIM])
    parts.append(w[:, o[7]:o[9]])
    main = jnp.concatenate(parts, axis=1)
    n_small = (o[4] - o[2]) + (o[11] - o[9])
    small = jnp.concatenate([w[:, o[2]:o[4]], w[:, o[9]:o[11]], jnp.zeros((w.shape[0], LANES - n_small), BF16)], axis=1)
    return main, small, w[:, o[11]:o[12]]


def _cparams(sem, vmem=VMEM_LIMIT):
    return pltpu.CompilerParams(dimension_semantics=sem, vmem_limit_bytes=vmem)


def _dot(a, b):
    return jnp.dot(a, b, preferred_element_type=F32)


def _dot_nt(a, b):
    return lax.dot_general(a, b, (((1,), (1,)), ((), ())), preferred_element_type=F32)


def _dot_tn(a, b):
    return lax.dot_general(a, b, (((0,), (0,)), ((), ())), preferred_element_type=F32)


def _split3(a):
    a1 = a.astype(BF16)
    r1 = a - a1.astype(F32)
    a2 = r1.astype(BF16)
    a3 = (r1 - a2.astype(F32)).astype(BF16)
    return a1, a2, a3


def _dot_exact_lhs(a, b01):
    a1, a2, a3 = _split3(a)
    return _dot(a1, b01) + (_dot(a2, b01) + _dot(a3, b01))


def _sigmoid(x):
    return 1.0 / (1.0 + jnp.exp(-x))


def _softplus(x):
    return jnp.maximum(x, 0.0) + jnp.log(1.0 + jnp.exp(-jnp.abs(x)))


def _iota2(shape, dim):
    return lax.broadcasted_iota(jnp.int32, shape, dim)


def _ln_rows(x, g, b):
    mu = jnp.mean(x, -1, keepdims=True)
    xc = x - mu
    var = jnp.mean(xc * xc, -1, keepdims=True)
    return xc * lax.rsqrt(var + LN_EPS) * g + b


def _ln_kernel(x_ref, g_ref, b_ref, o_ref, obf_ref):
    y = _ln_rows(x_ref[...], g_ref[...], b_ref[...])
    o_ref[...] = y
    obf_ref[...] = y.astype(BF16)


def _layer_norm_in(x2, g, b):
    t, d = x2.shape
    tm = ROW_BLOCK
    row = pl.BlockSpec((tm, d), lambda i: (i, 0))
    vec = pl.BlockSpec((1, d), lambda i: (0, 0))
    return pl.pallas_call(
        _ln_kernel, grid=(t // tm,), in_specs=[row, vec, vec], out_specs=[row, row],
        out_shape=[jax.ShapeDtypeStruct((t, d), F32), jax.ShapeDtypeStruct((t, d), BF16)],
        compiler_params=_cparams(("parallel",)), name="ln_in",
    )(x2, g.reshape(1, d), b.reshape(1, d))


def _matmul_kernel(x_ref, w_ref, o_ref):
    o_ref[...] = _dot(x_ref[...], w_ref[...]).astype(o_ref.dtype)


def _matmul(x, w, out_dtype, tm, tn, name):
    t, k = x.shape
    n = w.shape[1]
    return pl.pallas_call(
        _matmul_kernel, grid=(n // tn, t // tm),
        in_specs=[pl.BlockSpec((tm, k), lambda j, i: (i, 0)), pl.BlockSpec((k, tn), lambda j, i: (0, j))],
        out_specs=pl.BlockSpec((tm, tn), lambda j, i: (i, j)),
        out_shape=jax.ShapeDtypeStruct((t, n), out_dtype),
        compiler_params=_cparams(("parallel", "parallel")), name=name,
    )(x, w)


def _fill_halo(xs_ref, main, prev, nxt, i, nblk):
    ts = main.shape[0]
    xs_ref[0:8, :] = jnp.where(i > 0, prev[HALO - 8:HALO, :], 0.0)
    xs_ref[8:8 + ts, :] = main
    xs_ref[8 + ts:16 + ts, :] = jnp.where(i < nblk - 1, nxt[0:8, :], 0.0)


def _dn_prep_kernel(x_ref, xp_ref, xn_ref, cw_ref, o_ref, xs_ref):
    i = pl.program_id(1)
    j = pl.program_id(2)
    ts = x_ref.shape[1]
    _fill_halo(xs_ref, x_ref[0].astype(F32), xp_ref[0].astype(F32), xn_ref[0].astype(F32), i, pl.num_programs(1))
    pad = DN_CONV // 2
    for hd in range(DN_HEADS):
        cols = slice(hd * DN_DIM, (hd + 1) * DN_DIM)
        acc = jnp.zeros((ts, DN_DIM), F32)
        for w in range(DN_CONV):
            acc = acc + xs_ref[8 + w - pad:8 + w - pad + ts, cols] * cw_ref[w:w + 1, cols]
        y = acc * _sigmoid(acc)
        inv = lax.rsqrt(jnp.sum(y * y, -1, keepdims=True) + NORM_EPS)
        scale = jnp.where(j == 0, inv * DN_DIM ** -0.5, jnp.where(j == 1, inv, 1.0))
        o_ref[0, :, cols] = (y * scale).astype(BF16)


def _halo_specs(ts, width, col_of):
    r = ts // HALO

    def main(b, i, *rest):
        return (b, i, col_of(*rest))

    def prev(b, i, *rest):
        return (b, jnp.maximum(i * r - 1, 0), col_of(*rest))

    def make_next(nrow_blocks):
        def nxt(b, i, *rest):
            return (b, jnp.minimum((i + 1) * r, nrow_blocks - 1), col_of(*rest))
        return nxt

    return main, prev, make_next


def _dn_prep(proj3, conv_w):
    nb, ns, _ = proj3.shape
    ts = SEQ_BLOCK
    main, prev, make_next = _halo_specs(ts, MIX_WIDTH, lambda j: j)
    nxt = make_next(ns // HALO)
    return pl.pallas_call(
        _dn_prep_kernel, grid=(nb, ns // ts, 3),
        in_specs=[pl.BlockSpec((1, ts, MIX_WIDTH), main), pl.BlockSpec((1, HALO, MIX_WIDTH), prev),
                  pl.BlockSpec((1, HALO, MIX_WIDTH), nxt), pl.BlockSpec((DN_CONV, MIX_WIDTH), lambda b, i, j: (0, j))],
        out_specs=pl.BlockSpec((1, ts, MIX_WIDTH), main),
        out_shape=jax.ShapeDtypeStruct((nb, ns, 3 * MIX_WIDTH), BF16),
        scratch_shapes=[pltpu.VMEM((ts + 16, MIX_WIDTH), F32)],
        compiler_params=_cparams(("parallel", "parallel", "parallel")), name="dn_prep",
    )(proj3, proj3, proj3, conv_w)


def _chunk_masks(rev):
    r = _iota2((CHUNK, CHUNK), 0)
    c = _iota2((CHUNK, CHUNK), 1)
    if rev:
        return r <= c, r < c, r >= c
    return r >= c, r > c, r <= c


def _col_bcast2(row_a, row_b):
    r = _iota2((CHUNK, CHUNK), 0)
    c = _iota2((CHUNK, CHUNK), 1)
    eye = r == c
    diag = jnp.concatenate([jnp.where(eye, jnp.broadcast_to(row_a, (CHUNK, CHUNK)), 0.0),
                            jnp.where(eye, jnp.broadcast_to(row_b, (CHUNK, CHUNK)), 0.0)], axis=1)
    rr = _iota2((2 * CHUNK, 2 * LANES), 0)
    cc = _iota2((2 * CHUNK, 2 * LANES), 1)
    ones_bd = jnp.where((rr // CHUNK) == (cc // LANES), 1.0, 0.0).astype(BF16)
    out = _dot_exact_lhs(diag, ones_bd)
    return out[:, :LANES], out[:, LANES:]


def _bf(x):
    return x.astype(BF16)


def _each(items, fn):
    for it in items:
        fn(it)


def _tri_inverse_stages():
    r = _iota2((CHUNK, CHUNK), 0)
    c = _iota2((CHUNK, CHUNK), 1)
    same16 = (r // 16) == (c // 16)
    same32 = (r // 32) == (c // 32)

    def start(it):
        it['md'] = jnp.where(same16, it['m'], 0.0)
        md_bf = _bf(it['md'])
        it['n'] = -it['md']
        it['a'] = _dot(md_bf, md_bf)

    def square(it):
        a_bf = _bf(it['a'])
        both = _dot(jnp.concatenate([_bf(it['n']), a_bf], axis=0), a_bf)
        it['n'] = it['n'] + it['a'] + both[:CHUNK]
        it['a'] = both[CHUNK:]

    def last_power(it):
        it['n'] = it['n'] + it['a'] + _dot(_bf(it['n']), _bf(it['a']))

    def eliminate_a(mask_of):
        def fn(it):
            cpl = mask_of(it)
            it['x'] = cpl + _dot(_bf(it['n']), _bf(cpl))
        return fn

    def eliminate_b(it):
        it['n'] = it['n'] - it['x'] - _dot(_bf(it['x']), _bf(it['n']))

    return [start, square, square, last_power,
            eliminate_a(lambda it: jnp.where(same32, it['m'] - it['md'], 0.0)), eliminate_b,
            eliminate_a(lambda it: jnp.where(same32, 0.0, it['m'])), eliminate_b]


def _gdn_kernel(par_ref, qf_ref, kf_ref, vf_ref, qb_ref, kb_ref, vb_ref, gf_ref, gb_ref,
                of_ref, ob_ref, sf_ref, sb_ref):
    i = pl.program_id(1)
    nch = qf_ref.shape[1] // CHUNK

    @pl.when(i == 0)
    def _():
        sf_ref[...] = jnp.zeros_like(sf_ref)
        sb_ref[...] = jnp.zeros_like(sb_ref)

    def build(hd):
        cols = slice(hd * DN_DIM, (hd + 1) * DN_DIM)
        items = []
        scans = []
        for q_ref, k_ref, v_ref, g_ref, o_ref, s_ref, ch, rev in (
                (qf_ref, kf_ref, vf_ref, gf_ref, of_ref, sf_ref, hd, False),
                (qb_ref, kb_ref, vb_ref, gb_ref, ob_ref, sb_ref, DN_HEADS + hd, True)):
            incl, strict, cum = _chunk_masks(rev)
            cum01 = _bf(jnp.where(cum, 1.0, 0.0))
            a_rows = g_ref[0, ch]
            b_rows = g_ref[0, 2 * DN_HEADS + ch]
            g_rows = (-jnp.exp(jnp.full((nch, CHUNK), par_ref[ch], F32))
                      * _softplus(a_rows + par_ref[2 * DN_HEADS + ch]))
            beta_rows = _sigmoid(b_rows)
            gc_rows = _dot_exact_lhs(g_rows, cum01)
            g_tots = _dot_exact_lhs(g_rows, jnp.ones((CHUNK, LANES), BF16))
            order = []
            for c in (range(nch - 1, -1, -1) if rev else range(nch)):
                rows = slice(c * CHUNK, (c + 1) * CHUNK)
                it = dict(q_ref=q_ref, k_ref=k_ref, v_ref=v_ref, o_ref=o_ref, rows=rows, cols=cols, incl=incl,
                          strict=strict, gc_row=gc_rows[c:c + 1], beta_row=beta_rows[c:c + 1], g_tot=g_tots[c:c + 1])
                items.append(it)
                order.append(it)
            scans.append(dict(s_ref=s_ref, hd=hd, order=order))
        return items, scans

    def bcast(it):
        it['gc_col'], it['beta_col'] = _col_bcast2(it['gc_row'], it['beta_row'])

    def gram(it):
        q = it['q_ref'][0, it['rows'], it['cols']]
        k = it['k_ref'][0, it['rows'], it['cols']]
        decay = jnp.exp(jnp.where(it['incl'], it['gc_col'][:, :CHUNK] - it['gc_row'], NEG_BIG))
        kb = k.astype(F32) * it['beta_col']
        both = _dot_nt(jnp.concatenate([_bf(kb), q], axis=0), k)
        it['m'] = jnp.where(it['strict'], both[:CHUNK] * decay, 0.0)
        it['attn'] = _bf(both[CHUNK:] * decay)

    def solve(it):
        q = it['q_ref'][0, it['rows'], it['cols']]
        kf = it['k_ref'][0, it['rows'], it['cols']].astype(F32)
        v = it['v_ref'][0, it['rows'], it['cols']]
        e_col = jnp.exp(it['gc_col'])
        rhs = jnp.concatenate([v.astype(F32) * it['beta_col'], kf * it['beta_col'] * e_col], axis=1)
        sol = rhs + _dot(_bf(it['n']), _bf(rhs))
        it['u'] = sol[:, :DN_DIM]
        it['wq'] = jnp.concatenate([_bf(sol[:, DN_DIM:]), _bf(q.astype(F32) * e_col)], axis=0)
        it['kdec'] = _bf(kf * jnp.exp(it['g_tot'] - it['gc_col']))
        it['e_tot'] = jnp.exp(it['g_tot'])

    stages =[bcast, gram] + _tri_inverse_stages() + [solve]
    assert len(stages) >= nch

    def scan_step(scans, step):
        for sc in scans:
            if step == 0:
                sc['state'] = sc['s_ref'][sc['hd']]
            it = sc['order'][step]
            sc['ws'] = _dot(it['wq'], _bf(sc['state']))
        for sc in scans:
            it = sc['order'][step]
            sc['v_new'] = _bf(it['u'] - sc['ws'][:CHUNK])
            sc['state'] = sc['state'] * it['e_tot'] + _dot_tn(it['kdec'], sc['v_new'])
        for sc in scans:
            it = sc['order'][step]
            it['o_ref'][0, it['rows'], it['cols']] = sc['ws'][CHUNK:] + _dot(it['attn'], sc['v_new'])
            if step == nch - 1:
                sc['s_ref'][sc['hd']] = sc['state']

    pending = None
    for hd in range(DN_HEADS):
        items, scans = build(hd)
        for n_stage, stage in enumerate(stages):
            _each(items, stage)
            if pending is not None and n_stage < nch:
                scan_step(pending, n_stage)
        pending = scans
    for step in range(nch):
        scan_step(pending, step)


def _gdn(params, qkv, gates_t):
    nb, ns, _ = qkv.shape
    ts = SEQ_BLOCK
    nblk = ns // ts
    nch = ts // CHUNK

    def at(cblk, reverse):
        if reverse:
            return lambda b, i: (b, nblk - 1 - i, cblk)
        return lambda b, i: (b, i, cblk)

    blk = lambda cblk, reverse: pl.BlockSpec((1, ts, MIX_WIDTH), at(cblk, reverse))
    gspec = lambda reverse: pl.BlockSpec(
        (1, 32, nch, CHUNK), (lambda b, i: (b, 0, nblk - 1 - i, 0)) if reverse else (lambda b, i: (b, 0, i, 0)))
    out_sd = jax.ShapeDtypeStruct((nb, ns, MIX_WIDTH), F32)
    state = pltpu.VMEM((DN_HEADS, DN_DIM, DN_DIM), F32)
    return pl.pallas_call(
        _gdn_kernel, grid=(nb, nblk),
        in_specs=[pl.BlockSpec(memory_space=pltpu.SMEM),
                  blk(0, False), blk(1, False), blk(2, False), blk(0, True), blk(1, True), blk(2, True),
                  gspec(False), gspec(True)],
        out_specs=[blk(0, False), blk(0, True)],
        out_shape=[out_sd, out_sd],
        scratch_shapes=[state, state],
        compiler_params=_cparams(("parallel", "arbitrary")), name="gdn",
    )(params, qkv, qkv, qkv, qkv, qkv, qkv, gates_t, gates_t)


def _mlstm_kernel(par_ref, qkf_ref, vf_ref, qkb_ref, vb_ref, gf_ref, gb_ref, of_ref, ob_ref,
                  cf_ref, cb_ref, mf_ref, mb_ref):
    i = pl.program_id(1)
    nch = qkf_ref.shape[1] // CHUNK

    @pl.when(i == 0)
    def _():
        for ref in (cf_ref, cb_ref, mf_ref, mb_ref):
            ref[...] = jnp.zeros_like(ref)

    ones_col = _bf(jnp.where(_iota2((CHUNK, ML_V_DIM), 1) == 0, 1.0, 0.0))

    items = []
    scans = []
    for hd in range(ML_HEADS):
        for qk_ref, v_ref, g_ref, o_ref, c_ref, m_ref, ch, rev in (
                (qkf_ref, vf_ref, gf_ref, of_ref, cf_ref, mf_ref, hd, False),
                (qkb_ref, vb_ref, gb_ref, ob_ref, cb_ref, mb_ref, ML_HEADS + hd, True)):
            incl, _, cum = _chunk_masks(rev)
            cum01 = _bf(jnp.where(cum, 1.0, 0.0))
            i_rows = g_ref[0, 4 * DN_HEADS + ch] + par_ref[4 * DN_HEADS + ch]
            f_rows = g_ref[0, 4 * DN_HEADS + 2 * ML_HEADS + ch] + par_ref[4 * DN_HEADS + 2 * ML_HEADS + ch]
            logf = -_softplus(-f_rows)
            bc_rows = _dot_exact_lhs(logf, cum01)
            b_tots = _dot_exact_lhs(logf, jnp.ones((CHUNK, LANES), BF16))
            lw_rows = b_tots[:, :CHUNK] - bc_rows + i_rows
            a_maxs = jnp.max(lw_rows, axis=-1, keepdims=True)
            wgt_rows = jnp.exp(lw_rows - a_maxs)
            run = jnp.concatenate([i_rows - bc_rows, jnp.full((nch, LANES - CHUNK), NEG_BIG, F32)], axis=1)
            live = _iota2((nch, LANES), 1) < CHUNK
            for s in (1, 2, 4, 8, 16, 32):
                run = jnp.where(live, jnp.maximum(run, pltpu.roll(run, (LANES - s) if rev else s, 1)), NEG_BIG)
            dmax_rows = bc_rows + run[:, :CHUNK]
            order = []
            for c in (range(nch - 1, -1, -1) if rev else range(nch)):
                one = slice(c, c + 1)
                it = dict(qk_ref=qk_ref, v_ref=v_ref, o_ref=o_ref, rows=slice(c * CHUNK, (c + 1) * CHUNK),
                          cols=slice(hd * ML_V_DIM, (hd + 1) * ML_V_DIM), incl=incl,
                          bc_row=bc_rows[one], i_row=i_rows[one], wgt_row=wgt_rows[one], b_tot=b_tots[one],
                          a_max=a_maxs[one], dmax_row=dmax_rows[one])
                items.append(it)
                order.append(it)
            scans.append((c_ref, m_ref, hd, order))

    def load_qkv(it):
        qk = it['qk_ref'][0, it['rows'], it['cols']]
        q = qk[:, :ML_QK_DIM]
        kf = qk[:, ML_QK_DIM:].astype(F32) * ML_QK_DIM ** -0.5
        v_ext = jnp.concatenate([it['v_ref'][0, it['rows'], it['cols']], ones_col], axis=1)
        return q, kf, v_ext

    eye_qk = _bf(jnp.where(_iota2((ML_QK_DIM, ML_QK_DIM), 0) == _iota2((ML_QK_DIM, ML_QK_DIM), 1), 1.0, 0.0))

    def bcast(it):
        it['bc_col'], it['dmax_col'] = _col_bcast2(it['bc_row'], it['dmax_row'])

    def transpose_k(it):
        _, kf, _ = load_qkv(it)
        it['k_t'] = _dot_nt(eye_qk, _bf(kf))

    def local(it):
        q, kf, v_ext = load_qkv(it)
        it['cc'] = _dot(_bf(it['k_t'] * it['wgt_row']), v_ext)
        it['qk'] = _dot_nt(q, _bf(kf))

    _each(items, transpose_k)
    _each(items, bcast)
    _each(items, local)

    m_states = [m_ref[hd] for _, m_ref, hd, _ in scans]
    for step in range(nch):
        for n, (_, _, _, order) in enumerate(scans):
            it = order[step]
            it['m_prev'] = m_states[n]
            m_states[n] = jnp.maximum(it['b_tot'] + it['m_prev'], it['a_max'])
            it['m_new'] = m_states[n]
    for n, (_, m_ref, hd, _) in enumerate(scans):
        m_ref[hd] = m_states[n]

    def rescale(it):
        sp = jnp.exp(it['b_tot'] + it['m_prev'] - it['m_new'])
        sc = jnp.exp(it['a_max'] - it['m_new'])
        it['sp'] = jnp.concatenate([sp, sp], axis=1)
        it['cc'] = it['cc'] * jnp.concatenate([sc, sc], axis=1)

    _each(items, rescale)
    c_states = [c_ref[hd] for c_ref, _, hd, _ in scans]
    for step in range(nch):
        for n, (_, _, _, order) in enumerate(scans):
            it = order[step]
            it['c_prev'] = _bf(c_states[n])
            c_states[n] = c_states[n] * it['sp'] + it['cc']
    for n, (c_ref, _, hd, _) in enumerate(scans):
        c_ref[hd] = c_states[n]

    def output(it):
        q, _, v_ext = load_qkv(it)
        inter_log = it['bc_col'] + it['m_prev']
        m_out = jnp.maximum(inter_log, it['dmax_col'])
        d_log = jnp.where(it['incl'], it['bc_col'][:, :CHUNK] - it['bc_row'] + it['i_row'], NEG_BIG)
        s = it['qk'] * jnp.exp(d_log - m_out[:, :CHUNK])
        inter = jnp.exp(inter_log - m_out)
        num = jnp.concatenate([inter, inter], axis=1) * _dot(q, it['c_prev']) + _dot(_bf(s), v_ext)
        den = num[:, ML_V_DIM:ML_V_DIM + 1]
        it['o_ref'][0, it['rows'], it['cols']] = num[:, :ML_V_DIM] / jnp.maximum(jnp.abs(den), jnp.exp(-m_out))

    _each(items, output)


def _mlstm(params, proj3, gates_t):
    nb, ns, _ = proj3.shape
    ts = SEQ_BLOCK
    nblk = ns // ts
    nch = ts // CHUNK
    qk_blk, v_blk = 5, 6

    def at(cblk, reverse):
        if reverse:
            return lambda b, i: (b, nblk - 1 - i, cblk)
        return lambda b, i: (b, i, cblk)

    blk = lambda cblk, reverse: pl.BlockSpec((1, ts, MIX_WIDTH), at(cblk, reverse))
    gspec = lambda reverse: pl.BlockSpec(
        (1, 32, nch, CHUNK), (lambda b, i: (b, 0, nblk - 1 - i, 0)) if reverse else (lambda b, i: (b, 0, i, 0)))
    out_sd = jax.ShapeDtypeStruct((nb, ns, MIX_WIDTH), F32)
    c_state = pltpu.VMEM((ML_HEADS, ML_QK_DIM, 2 * ML_V_DIM), F32)
    m_state = pltpu.VMEM((ML_HEADS, 1, LANES), F32)
    return pl.pallas_call(
        _mlstm_kernel, grid=(nb, nblk),
        in_specs=[pl.BlockSpec(memory_space=pltpu.SMEM),
                  blk(qk_blk, False), blk(v_blk, False), blk(qk_blk, True), blk(v_blk, True),
                  gspec(False), gspec(True)],
        out_specs=[blk(0, False), blk(0, True)],
        out_shape=[out_sd, out_sd],
        scratch_shapes=[c_state, c_state, m_state, m_state],
        compiler_params=_cparams(("parallel", "arbitrary")), name="mlstm",
    )(params, proj3, proj3, proj3, proj3, gates_t, gates_t)


def _pool_kernel(u_ref, up_ref, un_ref, w_ref, sc_ref, o_ref, xs_ref):
    i = pl.program_id(1)
    nblk = pl.num_programs(1)
    ts = u_ref.shape[1]
    _fill_halo(xs_ref, u_ref[0].astype(F32), up_ref[0].astype(F32), un_ref[0].astype(F32), i, nblk)
    pos = i * ts + _iota2((ts, 1), 0)
    seq = nblk * ts
    for g, win in enumerate(POOL_WINDOWS):
        half = win // 2
        cols = slice(g * POOL_DIM, (g + 1) * POOL_DIM)
        acc = jnp.zeros((ts, POOL_DIM), F32)
        for d in range(-half, half):
            acc = acc + xs_ref[8 + d:8 + d + ts, cols]
        cnt = (jnp.minimum(pos + half, seq) - jnp.maximum(pos - half, 0)).astype(F32)
        p = acc / cnt - xs_ref[8:8 + ts, cols]
        y = _dot(p.astype(BF16), w_ref[g]) * sc_ref[:, cols]
        o_ref[0, :, cols] = y.astype(BF16)


def _pool(proj3, pool_w, pool_scale):
    nb, ns, _ = proj3.shape
    ts = SEQ_BLOCK
    cblk = 4
    main, prev, make_next = _halo_specs(ts, MIX_WIDTH, lambda: cblk)
    nxt = make_next(ns // HALO)
    return pl.pallas_call(
        _pool_kernel, grid=(nb, ns // ts),
        in_specs=[pl.BlockSpec((1, ts, MIX_WIDTH), main), pl.BlockSpec((1, HALO, MIX_WIDTH), prev),
                  pl.BlockSpec((1, HALO, MIX_WIDTH), nxt),
                  pl.BlockSpec((len(POOL_WINDOWS), POOL_DIM, POOL_DIM), lambda b, i: (0, 0, 0)),
                  pl.BlockSpec((1, MIX_WIDTH), lambda b, i: (0, 0))],
        out_specs=pl.BlockSpec((1, ts, MIX_WIDTH), lambda b, i: (b, i, 0)),
        out_shape=jax.ShapeDtypeStruct((nb, ns, MIX_WIDTH), BF16),
        scratch_shapes=[pltpu.VMEM((ts + 16, MIX_WIDTH), F32)],
        compiler_params=_cparams(("parallel", "parallel")), name="pool",
    )(proj3, proj3, proj3, pool_w, pool_scale)


def _merge_kernel(x_ref, h_ref, of_ref, ob_ref, z_ref, yp_ref, mf_ref, mb_ref, mo_ref,
                  wg_ref, wb_ref, wo_ref, dnw_ref, mlw_ref, lg_ref, lb_ref, o_ref, obf_ref):
    o_dn = of_ref[...] + ob_ref[...]
    h_ml = mf_ref[...] + mb_ref[...]
    dn_parts = []
    ml_parts = []
    for hd in range(DN_HEADS):
        cols = slice(hd * DN_DIM, (hd + 1) * DN_DIM)
        a = o_dn[:, cols]
        dn_parts.append(a * lax.rsqrt(jnp.mean(a * a, -1, keepdims=True) + NORM_EPS))
        c = h_ml[:, cols]
        c = c - jnp.mean(c, -1, keepdims=True)
        ml_parts.append(c * lax.rsqrt(jnp.mean(c * c, -1, keepdims=True) + NORM_EPS))
    z = z_ref[...].astype(F32)
    y_dn = jnp.concatenate(dn_parts, axis=1) * dnw_ref[...] * (z * _sigmoid(z))
    y_ml = jnp.concatenate(ml_parts, axis=1) * mlw_ref[...] * _sigmoid(mo_ref[...].astype(F32))
    branches = (y_dn.astype(BF16), yp_ref[...], y_ml.astype(BF16))
    x = x_ref[...]
    merged = jnp.zeros(o_ref.shape, F32)
    for r in range(N_BRANCH):
        gate = _sigmoid(_dot(x, wg_ref[:, r * D_MODEL:(r + 1) * D_MODEL]))
        merged = merged + gate * _dot(branches[r], wb_ref[r])
    mix = _dot(merged.astype(BF16), wo_ref[...])
    y = _ln_rows(ALPHA * h_ref[...] + mix, lg_ref[...], lb_ref[...])
    o_ref[...] = y
    obf_ref[...] = y.astype(BF16)


def _merge(x_bf, h, o_f, o_b, proj, y_pool, m_f, m_b, w_gate, w_branch, w_out, dn_norm_w, ml_norm_w, ln_g, ln_b):
    t, d = h.shape
    tm = ROW_BLOCK
    row = lambda width, cblk=0: pl.BlockSpec((tm, width), lambda i: (i, cblk))
    const = lambda shape: pl.BlockSpec(shape, lambda i: (0,) * len(shape), pipeline_mode=pl.Buffered(1))
    return pl.pallas_call(
        _merge_kernel, grid=(t // tm,),
        in_specs=[row(d), row(d), row(MIX_WIDTH), row(MIX_WIDTH), row(MIX_WIDTH, 3), row(MIX_WIDTH),
                  row(MIX_WIDTH), row(MIX_WIDTH), row(MIX_WIDTH, 7),
                  const((d, N_BRANCH * d)), const((N_BRANCH, MIX_WIDTH, d)), const((d, d)),
                  const((1, MIX_WIDTH)), const((1, MIX_WIDTH)), const((1, d)), const((1, d))],
        out_specs=[row(d), row(d)],
        out_shape=[jax.ShapeDtypeStruct((t, d), F32), jax.ShapeDtypeStruct((t, d), BF16)],
        compiler_params=_cparams(("parallel",)), name="merge",
    )(x_bf, h, o_f, o_b, proj, y_pool, m_f, m_b, proj, w_gate, w_branch, w_out,
      dn_norm_w.reshape(1, -1), ml_norm_w.reshape(1, -1), ln_g.reshape(1, -1), ln_b.reshape(1, -1))


def _route_kernel(h_ref, rw_ref, rb_ref, e_ref, gw_ref, rank_ref, cnt_ref, base_ref):
    i = pl.program_id(0)
    tm = h_ref.shape[0]
    G, J = N_GROUPS, EXPERTS_PER_GROUP

    @pl.when(i == 0)
    def _():
        base_ref[...] = jnp.zeros_like(base_ref)

    hv = h_ref[...]
    h_hi = _bf(hv)
    h_lo = _bf(hv - h_hi.astype(F32))
    rw = rw_ref[...]
    rw_hi = _bf(rw)
    rw_lo = _bf(rw - rw_hi.astype(F32))
    pre = _dot(h_hi, rw_hi) + (_dot(h_lo, rw_hi) + _dot(h_hi, rw_lo))
    sel_rows = jnp.where(_iota2((N_EXPERTS, LANES), 0) == _iota2((N_EXPERTS, LANES), 1), 1.0, 0.0).astype(BF16)
    p1, p2, p3 = _split3(pre)
    pre_t = _dot_nt(sel_rows, p1) + (_dot_nt(sel_rows, p2) + _dot_nt(sel_rows, p3))
    score = [_sigmoid(pre_t[j * G:(j + 1) * G, :]) for j in range(J)]
    sel = [score[j] + rb_ref[j][:, :1] for j in range(J)]
    pair = None
    for a in range(J):
        for b in range(a + 1, J):
            s = sel[a] + sel[b]
            pair = s if pair is None else jnp.maximum(pair, s)
    gid = _iota2((G, tm), 0)
    gmax = jnp.max(pair, axis=0, keepdims=True)
    grp = jnp.min(jnp.where(pair == gmax, gid.astype(F32), float(G)), axis=0, keepdims=True).astype(jnp.int32)
    in_grp = gid == grp
    pick = lambda arr: jnp.sum(jnp.where(in_grp, arr, 0.0), axis=0, keepdims=True)
    v = [pick(sel[j]) for j in range(J)]
    u = [pick(score[j]) for j in range(J)]
    rank = []
    for j in range(J):
        rj = jnp.zeros((1, tm), jnp.int32)
        for a in range(J):
            if a == j:
                continue
            ahead = (v[a] >= v[j]) if a < j else (v[a] > v[j])
            rj = rj + ahead.astype(jnp.int32)
        rank.append(rj)
    loc = []
    wsel = []
    for kk in range(TOP_K):
        lk = jnp.zeros((1, tm), jnp.int32)
        wk = jnp.zeros((1, tm), F32)
        for j in range(J):
            hit = rank[j] == kk
            lk = lk + jnp.where(hit, j, 0)
            wk = wk + jnp.where(hit, u[j], 0.0)
        loc.append(lk)
        wsel.append(wk)
    wsum = wsel[0] + wsel[1]
    e_ref[...] = jnp.concatenate([grp * J + loc[0], grp * J + loc[1]], axis=0)
    gw_ref[...] = jnp.concatenate([wsel[0] / wsum, wsel[1] / wsum], axis=0)
    before = jnp.where(_iota2((tm, tm), 0) < _iota2((tm, tm), 1), 1.0, 0.0).astype(BF16)
    ones = jnp.ones((tm, LANES), BF16)
    r_k = [jnp.zeros((1, tm), F32) for _ in range(TOP_K)]
    for j in range(J):
        hits = [in_grp & (loc[kk] == j) for kk in range(TOP_K)]
        oh = jnp.where(hits[0] | hits[1], 1.0, 0.0)
        pos = _dot(oh.astype(BF16), before) + base_ref[j][:, :1]
        for kk in range(TOP_K):
            r_k[kk] = r_k[kk] + jnp.sum(jnp.where(hits[kk], pos, 0.0), axis=0, keepdims=True)
        base_ref[j] = base_ref[j] + _dot(oh.astype(BF16), ones)
    rank_ref[...] = jnp.concatenate(r_k, axis=0).astype(jnp.int32)
    cnt_ref[...] = base_ref[...]


def _route(h, router_w, router_b):
    t, d = h.shape
    tm = SEQ_BLOCK
    G, J = N_GROUPS, EXPERTS_PER_GROUP
    perm = np.asarray([g * J + j for j in range(J) for g in range(G)])
    rw = jnp.zeros((d, LANES), F32).at[:, :N_EXPERTS].set(router_w.astype(F32)[:, perm])
    rb = jnp.broadcast_to(router_b.astype(F32)[perm].reshape(J, G, 1), (J, G, LANES))
    pair = lambda dt: jax.ShapeDtypeStruct((TOP_K, t), dt)
    tok = pl.BlockSpec((TOP_K, tm), lambda i: (0, i))
    return pl.pallas_call(
        _route_kernel, grid=(t // tm,),
        in_specs=[pl.BlockSpec((tm, d), lambda i: (i, 0)), pl.BlockSpec((d, LANES), lambda i: (0, 0)),
                  pl.BlockSpec((J, G, LANES), lambda i: (0, 0, 0))],
        out_specs=[tok, tok, tok, pl.BlockSpec((J, G, LANES), lambda i: (0, 0, 0))],
        out_shape=[pair(jnp.int32), pair(F32), pair(jnp.int32), jax.ShapeDtypeStruct((J, G, LANES), F32)],
        scratch_shapes=[pltpu.VMEM((J, G, LANES), F32)],
        compiler_params=_cparams(("arbitrary",)), name="route",
    )(h, rw, rb)


def _dispatch_kernel(pend_ref, dest_ref, h_ref, xb_ref, zero_ref, sem, zsem):
    tm = h_ref.shape[0]

    @pl.when(pl.program_id(0) == 0)
    def _():
        zero_ref[...] = jnp.zeros_like(zero_ref)

        def fill(start):
            if not isinstance(start, int):
                start = pl.multiple_of(start, MOE_ROWS)
            return pltpu.make_async_copy(zero_ref, xb_ref.at[pl.ds(start, MOE_ROWS)], zsem)

        n_rows = xb_ref.shape[0]
        jobs = []
        for e in range(N_EXPERTS):
            jobs.append((pend_ref[e] > (pend_ref[e - 1] if e > 0 else 0), pend_ref[e] - MOE_ROWS))
            tail = n_rows - (e + 1) * MOE_ROWS
            jobs.append((tail >= pend_ref[N_EXPERTS - 1], tail))
        for cond, start in jobs:
            pl.when(cond)(lambda start=start: fill(start).start())
        for cond, start in jobs:
            pl.when(cond)(lambda start=start: fill(start).wait())

    def row_copy(t, kk):
        return pltpu.make_async_copy(h_ref.at[pl.ds(t, 1)], xb_ref.at[pl.ds(dest_ref[kk, t], 1)], sem)

    def start(t, carry):
        for kk in range(TOP_K):
            row_copy(t, kk).start()
        return carry

    def wait(t, carry):
        for kk in range(TOP_K):
            row_copy(t, kk).wait()
        return carry

    lax.fori_loop(0, tm, start, 0, unroll=8)
    lax.fori_loop(0, tm, wait, 0, unroll=8)


def _dispatch(pend, dest, h, n_rows):
    t, d = h.shape
    tm = DISPATCH_BLOCK
    grid_spec = pltpu.PrefetchScalarGridSpec(
        num_scalar_prefetch=1, grid=(t // tm,),
        in_specs=[pl.BlockSpec((TOP_K, tm), lambda i, pe: (0, i), memory_space=pltpu.SMEM),
                  pl.BlockSpec((tm, d), lambda i, pe: (i, 0))],
        out_specs=pl.BlockSpec(memory_space=pl.ANY),
        scratch_shapes=[pltpu.VMEM((MOE_ROWS, d), F32), pltpu.SemaphoreType.DMA(()), pltpu.SemaphoreType.DMA(())])
    return pl.pallas_call(
        _dispatch_kernel, grid_spec=grid_spec, out_shape=jax.ShapeDtypeStruct((n_rows, d), F32),
        compiler_params=_cparams(("arbitrary",)), name="moe_dispatch",
    )(pend, dest, h)


def _expert_kernel(be_ref, nact_ref, x_ref, w1_ref, w3_ref, w2_ref, o_ref):
    i = pl.program_id(0)

    @pl.when(i < nact_ref[0])
    def _():
        x = x_ref[...].astype(BF16)
        a = _dot(x, w1_ref[0])
        b = _dot(x, w3_ref[0])
        act = a * _sigmoid(a) * b
        o_ref[...] = _dot(act.astype(BF16), w2_ref[0])

    @pl.when(i >= nact_ref[0])
    def _():
        o_ref[...] = jnp.zeros_like(o_ref)


def _experts(block_expert, n_active, xb, w1, w3, w2):
    n_rows, d = xb.shape
    bm = MOE_ROWS
    grid_spec = pltpu.PrefetchScalarGridSpec(
        num_scalar_prefetch=2, grid=(n_rows // bm,),
        in_specs=[pl.BlockSpec((bm, d), lambda i, be, na: (jnp.minimum(i, na[0] - 1), 0)),
                  pl.BlockSpec((1, d, D_EXPERT), lambda i, be, na: (be[i], 0, 0)),
                  pl.BlockSpec((1, d, D_EXPERT), lambda i, be, na: (be[i], 0, 0)),
                  pl.BlockSpec((1, D_EXPERT, d), lambda i, be, na: (be[i], 0, 0))],
        out_specs=pl.BlockSpec((bm, d), lambda i, be, na: (i, 0)))
    return pl.pallas_call(
        _expert_kernel, grid_spec=grid_spec, out_shape=jax.ShapeDtypeStruct((n_rows, d), F32),
        compiler_params=_cparams(("arbitrary",)), name="moe_experts",
    )(block_expert, n_active, xb, w1, w3, w2)


def _combine_kernel(dest_ref, dest_next_ref, h_ref, gw_ref, yb_ref, lg_ref, lb_ref, o_ref, obf_ref, buf_ref, sem):
    i = pl.program_id(0)
    tm = h_ref.shape[0]
    slot = lax.rem(i, 2)

    def gather(idx_ref, s):
        def start(t, carry):
            for kk in range(TOP_K):
                pltpu.make_async_copy(yb_ref.at[pl.ds(idx_ref[kk, t], 1)], buf_ref.at[s, kk, pl.ds(t, 1)],
                                      sem.at[s]).start()
            return carry
        lax.fori_loop(0, tm, start, 0, unroll=8)

    @pl.when(i == 0)
    def _():
        gather(dest_ref, 0)

    @pl.when(i + 1 < pl.num_programs(0))
    def _():
        gather(dest_next_ref, 1 - slot)

    def wait(t, carry):
        for kk in range(TOP_K):
            pltpu.make_async_copy(yb_ref.at[pl.ds(0, 1)], buf_ref.at[slot, kk, pl.ds(t, 1)], sem.at[slot]).wait()
        return carry

    lax.fori_loop(0, tm, wait, 0, unroll=8)
    gw = gw_ref[...]
    ffn = gw[:, 0:1] * buf_ref[slot, 0] + gw[:, 1:2] * buf_ref[slot, 1]
    y = _ln_rows(ALPHA * h_ref[...] + ffn, lg_ref[...], lb_ref[...])
    o_ref[...] = y
    obf_ref[...] = y.astype(BF16)


def _combine(dest, h, gw_cols, yb, ln_g, ln_b):
    t, d = h.shape
    tm = COMBINE_BLOCK
    row = pl.BlockSpec((tm, d), lambda i: (i, 0))
    vec = pl.BlockSpec((1, d), lambda i: (0, 0))
    nsteps = t // tm
    return pl.pallas_call(
        _combine_kernel, grid=(nsteps,),
        in_specs=[pl.BlockSpec((TOP_K, tm), lambda i: (0, i), memory_space=pltpu.SMEM),
                  pl.BlockSpec((TOP_K, tm), lambda i: (0, jnp.minimum(i + 1, nsteps - 1)), memory_space=pltpu.SMEM),
                  row, pl.BlockSpec((tm, TOP_K), lambda i: (i, 0)), pl.BlockSpec(memory_space=pl.ANY), vec, vec],
        out_specs=[row, row],
        out_shape=[jax.ShapeDtypeStruct((t, d), F32), jax.ShapeDtypeStruct((t, d), BF16)],
        scratch_shapes=[pltpu.VMEM((2, TOP_K, tm, d), F32), pltpu.SemaphoreType.DMA((2,))],
        compiler_params=_cparams(("arbitrary",)), name="moe_combine",
    )(dest, dest, h, gw_cols, yb, ln_g.reshape(1, d), ln_b.reshape(1, d))


def _moe(h, router_w, router_b, w1, w3, w2, ln_g, ln_b):
    t, d = h.shape
    expert, gate_w, rank, counts = _route(h, router_w, router_b)
    cnt = counts[:, :, 0].T.reshape(N_EXPERTS).astype(jnp.int32)
    padded = ((cnt + MOE_ROWS - 1) // MOE_ROWS) * MOE_ROWS
    pend = jnp.cumsum(padded)
    pstart = pend - padded
    is_e = expert[..., None] == jnp.arange(N_EXPERTS, dtype=jnp.int32)
    dest = jnp.sum(jnp.where(is_e, pstart.astype(jnp.int32), 0), axis=-1) + rank
    n_rows = t * TOP_K + N_EXPERTS * MOE_ROWS
    n_blocks = n_rows // MOE_ROWS
    block_start = jnp.arange(n_blocks, dtype=jnp.int32) * MOE_ROWS
    block_expert = jnp.minimum(jnp.sum((pend[None, :] <= block_start[:, None]).astype(jnp.int32), axis=1), N_EXPERTS - 1)
    n_active = (pend[-1:] // MOE_ROWS).astype(jnp.int32)
    xb = _dispatch(pend.astype(jnp.int32), dest, h, n_rows)
    yb = _experts(block_expert, n_active, xb, w1, w3, w2)
    return _combine(dest, h, gate_w.T, yb, ln_g, ln_b)


def kernel(x, ln_in_g, ln_in_b, w_in, dn_conv_w, dn_a_log, dn_dt_bias, dn_norm_w, pool_w, pool_scale,
           ml_gate_b, ml_norm_w, w_branch, w_out, ln_mix_g, ln_mix_b, router_w, router_b,
           moe_w1, moe_w3, moe_w2, ln_ffn_g, ln_ffn_b):
    nb, ns, d = x.shape
    t = nb * ns
    h, h_bf = _layer_norm_in(x.reshape(t, d), ln_in_g, ln_in_b)
    for l in range(DEPTH):
        w_main, w_small, w_gate = _split_in_proj(w_in[l])
        proj = _matmul(h_bf, w_main, BF16, 512, 1024, "in_proj")
        small = _matmul(h_bf, w_small, F32, 512, LANES, "in_proj_gates")
        proj3 = proj.reshape(nb, ns, MAIN_WIDTH)
        gates_t = small[:, :32].reshape(nb, ns // CHUNK, CHUNK, 32).transpose(0, 3, 1, 2)
        params = jnp.concatenate([dn_a_log[l], dn_dt_bias[l], ml_gate_b[l]]).astype(F32)
        qkv = _dn_prep(proj3, dn_conv_w[l].astype(F32))
        o_f, o_b = _gdn(params, qkv, gates_t)
        y_pool = _pool(proj3, pool_w[l].astype(BF16), pool_scale[l].astype(F32).reshape(1, -1))
        m_f, m_b = _mlstm(params, proj3, gates_t)
        flat = lambda a: a.reshape(t, -1)
        h, h_bf = _merge(h_bf, h, flat(o_f), flat(o_b), proj, flat(y_pool), flat(m_f), flat(m_b),
                         w_gate, w_branch[l].astype(BF16), w_out[l].astype(BF16),
                         dn_norm_w[l].astype(F32), ml_norm_w[l].astype(F32), ln_mix_g[l], ln_mix_b[l])
        h, h_bf = _moe(h, router_w, router_b, moe_w1[l].astype(BF16), moe_w3[l].astype(BF16),
                       moe_w2[l].astype(BF16), ln_ffn_g[l], ln_ffn_b[l])
    return h.reshape(nb, ns, d)
```

```python
import jax
import jax.numpy as jnp
import numpy as np
from jax import lax
from jax.experimental import pallas as pl
from jax.experimental.pallas import tpu as pltpu

F32 = jnp.float32
BF16 = jnp.bfloat16

D_MODEL = 1024
DEPTH = 2
MIX_WIDTH = D_MODEL // 2
DN_HEADS = 4
DN_DIM = MIX_WIDTH // DN_HEADS
DN_CONV = 5
CHUNK = 64
POOL_WINDOWS = (2, 4, 8, 16)
POOL_DIM = MIX_WIDTH // len(POOL_WINDOWS)
ML_HEADS = 4
ML_V_DIM = MIX_WIDTH // ML_HEADS
ML_QK_DIM = ML_V_DIM // 2
N_BRANCH = 3
N_EXPERTS = 32
N_GROUPS = 8
EXPERTS_PER_GROUP = N_EXPERTS // N_GROUPS
TOP_K = 2
D_EXPERT = D_MODEL // 2
LN_EPS = 1e-5
NORM_EPS = 1e-6
ALPHA = (2.0 * DEPTH) ** 0.25

LANES = 128
HALO = 16
SEQ_BLOCK = 512
ROW_BLOCK = 512
MOE_ROWS = 256
DISPATCH_BLOCK = 256
COMBINE_BLOCK = 128
NEG_BIG = -1e30
VMEM_LIMIT = 56 * 1024 * 1024

_SPLITS = (3 * MIX_WIDTH, MIX_WIDTH, 2 * DN_HEADS, 2 * DN_HEADS, MIX_WIDTH, ML_HEADS * ML_QK_DIM,
           ML_HEADS * ML_QK_DIM, MIX_WIDTH, MIX_WIDTH, 2 * ML_HEADS, 2 * ML_HEADS, N_BRANCH * D_MODEL)
_OFF = [0] + [int(c) for c in np.cumsum(_SPLITS)]
MAIN_WIDTH = 8 * MIX_WIDTH


def _split_in_proj(w):
    o = _OFF
    w = w.astype(BF16)
    parts = [w[:, o[0]:o[2]], w[:, o[4]:o[5]]]
    for h in range(ML_HEADS):
        parts.append(w[:, o[5] + h * ML_QK_DIM:o[5] + (h + 1) * ML_QK_DIM])
        parts.append(w[:, o[6] + h * ML_QK_DIM:o[6] + (h + 1) * ML_QK_DIM])
    parts.append(w[:, o[7]:o[9]])
    main = jnp.concatenate(parts, axis=1)
    n_small = (o[4] - o[2]) + (o[11] - o[9])
    small = jnp.concatenate([w[:, o[2]:o[4]], w[:, o[9]:o[11]], jnp.zeros((w.shape[0], LANES - n_small), BF16)], axis=1)
    return main, small, w[:, o[11]:o[12]]


def _cparams(sem, vmem=VMEM_LIMIT):
    return pltpu.CompilerParams(dimension_semantics=sem, vmem_limit_bytes=vmem)


def _dot(a, b):
    return jnp.dot(a, b, preferred_element_type=F32)


def _dot_nt(a, b):
    return lax.dot_general(a, b, (((1,), (1,)), ((), ())), preferred_element_type=F32)


def _dot_tn(a, b):
    return lax.dot_general(a, b, (((0,), (0,)), ((), ())), preferred_element_type=F32)


def _split3(a):
    a1 = a.astype(BF16)
    r1 = a - a1.astype(F32)
    a2 = r1.astype(BF16)
    a3 = (r1 - a2.astype(F32)).astype(BF16)
    return a1, a2, a3


def _dot_exact_lhs(a, b01):
    a1, a2, a3 = _split3(a)
    return _dot(a1, b01) + (_dot(a2, b01) + _dot(a3, b01))


def _sigmoid(x):
    return 1.0 / (1.0 + jnp.exp(-x))


def _softplus(x):
    return jnp.maximum(x, 0.0) + jnp.log(1.0 + jnp.exp(-jnp.abs(x)))


def _iota2(shape, dim):
    return lax.broadcasted_iota(jnp.int32, shape, dim)


def _ln_rows(x, g, b):
    mu = jnp.mean(x, -1, keepdims=True)
    xc = x - mu
    var = jnp.mean(xc * xc, -1, keepdims=True)
    return xc * lax.rsqrt(var + LN_EPS) * g + b


def _ln_kernel(x_ref, g_ref, b_ref, o_ref, obf_ref):
    y = _ln_rows(x_ref[...], g_ref[...], b_ref[...])
    o_ref[...] = y
    obf_ref[...] = y.astype(BF16)


def _layer_norm_in(x2, g, b):
    t, d = x2.shape
    tm = ROW_BLOCK
    row = pl.BlockSpec((tm, d), lambda i: (i, 0))
    vec = pl.BlockSpec((1, d), lambda i: (0, 0))
    return pl.pallas_call(
        _ln_kernel, grid=(t // tm,), in_specs=[row, vec, vec], out_specs=[row, row],
        out_shape=[jax.ShapeDtypeStruct((t, d), F32), jax.ShapeDtypeStruct((t, d), BF16)],
        compiler_params=_cparams(("parallel",)), name="ln_in",
    )(x2, g.reshape(1, d), b.reshape(1, d))


def _matmul_kernel(x_ref, w_ref, o_ref):
    o_ref[...] = _dot(x_ref[...], w_ref[...]).astype(o_ref.dtype)


def _matmul(x, w, out_dtype, tm, tn, name):
    t, k = x.shape
    n = w.shape[1]
    return pl.pallas_call(
        _matmul_kernel, grid=(n // tn, t // tm),
        in_specs=[pl.BlockSpec((tm, k), lambda j, i: (i, 0)), pl.BlockSpec((k, tn), lambda j, i: (0, j))],
        out_specs=pl.BlockSpec((tm, tn), lambda j, i: (i, j)),
        out_shape=jax.ShapeDtypeStruct((t, n), out_dtype),
        compiler_params=_cparams(("parallel", "parallel")), name=name,
    )(x, w)


def _fill_halo(xs_ref, main, prev, nxt, i, nblk):
    ts = main.shape[0]
    xs_ref[0:8, :] = jnp.where(i > 0, prev[HALO - 8:HALO, :], 0.0)
    xs_ref[8:8 + ts, :] = main
    xs_ref[8 + ts:16 + ts, :] = jnp.where(i < nblk - 1, nxt[0:8, :], 0.0)


def _dn_prep_kernel(x_ref, xp_ref, xn_ref, cw_ref, o_ref, xs_ref):
    i = pl.program_id(1)
    j = pl.program_id(2)
    ts = x_ref.shape[1]
    _fill_halo(xs_ref, x_ref[0].astype(F32), xp_ref[0].astype(F32), xn_ref[0].astype(F32), i, pl.num_programs(1))
    pad = DN_CONV // 2
    for hd in range(DN_HEADS):
        cols = slice(hd * DN_DIM, (hd + 1) * DN_DIM)
        acc = jnp.zeros((ts, DN_DIM), F32)
        for w in range(DN_CONV):
            acc = acc + xs_ref[8 + w - pad:8 + w - pad + ts, cols] * cw_ref[w:w + 1, cols]
        y = acc * _sigmoid(acc)
        inv = lax.rsqrt(jnp.sum(y * y, -1, keepdims=True) + NORM_EPS)
        scale = jnp.where(j == 0, inv * DN_DIM ** -0.5, jnp.where(j == 1, inv, 1.0))
        o_ref[0, :, cols] = (y * scale).astype(BF16)


def _halo_specs(ts, width, col_of):
    r = ts // HALO

    def main(b, i, *rest):
        return (b, i, col_of(*rest))

    def prev(b, i, *rest):
        return (b, jnp.maximum(i * r - 1, 0), col_of(*rest))

    def make_next(nrow_blocks):
        def nxt(b, i, *rest):
            return (b, jnp.minimum((i + 1) * r, nrow_blocks - 1), col_of(*rest))
        return nxt

    return main, prev, make_next


def _dn_prep(proj3, conv_w):
    nb, ns, _ = proj3.shape
    ts = SEQ_BLOCK
    main, prev, make_next = _halo_specs(ts, MIX_WIDTH, lambda j: j)
    nxt = make_next(ns // HALO)
    return pl.pallas_call(
        _dn_prep_kernel, grid=(nb, ns // ts, 3),
        in_specs=[pl.BlockSpec((1, ts, MIX_WIDTH), main), pl.BlockSpec((1, HALO, MIX_WIDTH), prev),
                  pl.BlockSpec((1, HALO, MIX_WIDTH), nxt), pl.BlockSpec((DN_CONV, MIX_WIDTH), lambda b, i, j: (0, j))],
        out_specs=pl.BlockSpec((1, ts, MIX_WIDTH), main),
        out_shape=jax.ShapeDtypeStruct((nb, ns, 3 * MIX_WIDTH), BF16),
        scratch_shapes=[pltpu.VMEM((ts + 16, MIX_WIDTH), F32)],
        compiler_params=_cparams(("parallel", "parallel", "parallel")), name="dn_prep",
    )(proj3, proj3, proj3, conv_w)


def _chunk_masks(rev):
    r = _iota2((CHUNK, CHUNK), 0)
    c = _iota2((CHUNK, CHUNK), 1)
    if rev:
        return r <= c, r < c, r >= c
    return r >= c, r > c, r <= c


def _col_bcast2(row_a, row_b):
    r = _iota2((CHUNK, CHUNK), 0)
    c = _iota2((CHUNK, CHUNK), 1)
    eye = r == c
    diag = jnp.concatenate([jnp.where(eye, jnp.broadcast_to(row_a, (CHUNK, CHUNK)), 0.0),
                            jnp.where(eye, jnp.broadcast_to(row_b, (CHUNK, CHUNK)), 0.0)], axis=1)
    rr = _iota2((2 * CHUNK, 2 * LANES), 0)
    cc = _iota2((2 * CHUNK, 2 * LANES), 1)
    ones_bd = jnp.where((rr // CHUNK) == (cc // LANES), 1.0, 0.0).astype(BF16)
    hi = _bf(diag)
    lo = _bf(diag - hi.astype(F32))
    out = _dot(hi, ones_bd) + _dot(lo, ones_bd)
    return out[:, :LANES], out[:, LANES:]


def _bf(x):
    return x.astype(BF16)


def _each(items, fn):
    for it in items:
        fn(it)


def _tri_inverse_stages():
    r = _iota2((CHUNK, CHUNK), 0)
    c = _iota2((CHUNK, CHUNK), 1)
    same16 = (r // 16) == (c // 16)
    same32 = (r // 32) == (c // 32)

    def start(it):
        it['md'] = jnp.where(same16, it['m'], 0.0)
        md_bf = _bf(it['md'])
        it['n'] = -it['md']
        it['a'] = _dot(md_bf, md_bf)

    def square(it):
        a_bf = _bf(it['a'])
        both = _dot(jnp.concatenate([_bf(it['n']), a_bf], axis=0), a_bf)
        it['n'] = it['n'] + it['a'] + both[:CHUNK]
        it['a'] = both[CHUNK:]

    def last_power(it):
        it['n'] = it['n'] + it['a'] + _dot(_bf(it['n']), _bf(it['a']))

    def eliminate_a(mask_of):
        def fn(it):
            cpl = mask_of(it)
            it['x'] = cpl + _dot(_bf(it['n']), _bf(cpl))
        return fn

    def eliminate_b(it):
        it['n'] = it['n'] - it['x'] - _dot(_bf(it['x']), _bf(it['n']))

    return [start, square, square, last_power,
            eliminate_a(lambda it: jnp.where(same32, it['m'] - it['md'], 0.0)), eliminate_b,
            eliminate_a(lambda it: jnp.where(same32, 0.0, it['m'])), eliminate_b]


def _gdn_kernel(par_ref, qf_ref, kf_ref, vf_ref, qb_ref, kb_ref, vb_ref, gf_ref, gb_ref,
                of_ref, ob_ref, sf_ref, sb_ref):
    i = pl.program_id(1)
    nch = qf_ref.shape[1] // CHUNK

    @pl.when(i == 0)
    def _():
        sf_ref[...] = jnp.zeros_like(sf_ref)
        sb_ref[...] = jnp.zeros_like(sb_ref)

    def build(hd):
        cols = slice(hd * DN_DIM, (hd + 1) * DN_DIM)
        items = []
        scans = []
        for q_ref, k_ref, v_ref, g_ref, o_ref, s_ref, ch, rev in (
                (qf_ref, kf_ref, vf_ref, gf_ref, of_ref, sf_ref, hd, False),
                (qb_ref, kb_ref, vb_ref, gb_ref, ob_ref, sb_ref, DN_HEADS + hd, True)):
            incl, strict, cum = _chunk_masks(rev)
            cum01 = _bf(jnp.where(cum, 1.0, 0.0))
            a_rows = g_ref[0, ch]
            b_rows = g_ref[0, 2 * DN_HEADS + ch]
            g_rows = (-jnp.exp(jnp.full((nch, CHUNK), par_ref[ch], F32))
                      * _softplus(a_rows + par_ref[2 * DN_HEADS + ch]))
            beta_rows = _sigmoid(b_rows)
            gc_rows = _dot_exact_lhs(g_rows, cum01)
            g_tots = _dot_exact_lhs(g_rows, jnp.ones((CHUNK, LANES), BF16))
            order = []
            for c in (range(nch - 1, -1, -1) if rev else range(nch)):
                rows = slice(c * CHUNK, (c + 1) * CHUNK)
                it = dict(q_ref=q_ref, k_ref=k_ref, v_ref=v_ref, o_ref=o_ref, rows=rows, cols=cols, incl=incl,
                          strict=strict, gc_row=gc_rows[c:c + 1], beta_row=beta_rows[c:c + 1], g_tot=g_tots[c:c + 1])
                items.append(it)
                order.append(it)
            scans.append(dict(s_ref=s_ref, hd=hd, order=order))
        return items, scans

    def bcast(it):
        it['gc_col'], it['beta_col'] = _col_bcast2(it['gc_row'], it['beta_row'])

    def gram(it):
        q = it['q_ref'][0, it['rows'], it['cols']]
        k = it['k_ref'][0, it['rows'], it['cols']]
        decay = jnp.exp(jnp.where(it['incl'], it['gc_col'][:, :CHUNK] - it['gc_row'], NEG_BIG))
        kb = k.astype(F32) * it['beta_col']
        both = _dot_nt(jnp.concatenate([_bf(kb), q], axis=0), k)
        it['m'] = jnp.where(it['strict'], both[:CHUNK] * decay, 0.0)
        it['attn'] = _bf(both[CHUNK:] * decay)

    def solve(it):
        q = it['q_ref'][0, it['rows'], it['cols']]
        kf = it['k_ref'][0, it['rows'], it['cols']].astype(F32)
        v = it['v_ref'][0, it['rows'], it['cols']]
        e_col = jnp.exp(it['gc_col'])
        rhs = jnp.concatenate([v.astype(F32) * it['beta_col'], kf * it['beta_col'] * e_col], axis=1)
        sol = rhs + _dot(_bf(it['n']), _bf(rhs))
        it['u'] = sol[:, :DN_DIM]
        it['wq'] = jnp.concatenate([_bf(sol[:, DN_DIM:]), _bf(q.astype(F32) * e_col)], axis=0)
        it['kdec'] = _bf(kf * jnp.exp(it['g_tot'] - it['gc_col']))
        it['e_tot'] = jnp.exp(it['g_tot'])

    stages =[bcast, gram] + _tri_inverse_stages() + [solve]
    assert len(stages) >= nch

    def scan_step(scans, step):
        for sc in scans:
            if step == 0:
                sc['state'] = sc['s_ref'][sc['hd']]
            it = sc['order'][step]
            sc['ws'] = _dot(it['wq'], _bf(sc['state']))
        for sc in scans:
            it = sc['order'][step]
            sc['v_new'] = _bf(it['u'] - sc['ws'][:CHUNK])
            sc['state'] = sc['state'] * it['e_tot'] + _dot_tn(it['kdec'], sc['v_new'])
        for sc in scans:
            it = sc['order'][step]
            it['o_ref'][0, it['rows'], it['cols']] = sc['ws'][CHUNK:] + _dot(it['attn'], sc['v_new'])
            if step == nch - 1:
                sc['s_ref'][sc['hd']] = sc['state']

    pending = None
    for hd in range(DN_HEADS):
        items, scans = build(hd)
        for n_stage, stage in enumerate(stages):
            _each(items, stage)
            if pending is not None and n_stage < nch:
                scan_step(pending, n_stage)
        pending = scans
    for step in range(nch):
        scan_step(pending, step)


def _gdn(params, qkv, gates_t):
    nb, ns, _ = qkv.shape
    ts = SEQ_BLOCK
    nblk = ns // ts
    nch = ts // CHUNK

    def at(cblk, reverse):
        if reverse:
            return lambda b, i: (b, nblk - 1 - i, cblk)
        return lambda b, i: (b, i, cblk)

    blk = lambda cblk, reverse: pl.BlockSpec((1, ts, MIX_WIDTH), at(cblk, reverse))
    gspec = lambda reverse: pl.BlockSpec(
        (1, 32, nch, CHUNK), (lambda b, i: (b, 0, nblk - 1 - i, 0)) if reverse else (lambda b, i: (b, 0, i, 0)))
    out_sd = jax.ShapeDtypeStruct((nb, ns, MIX_WIDTH), F32)
    state = pltpu.VMEM((DN_HEADS, DN_DIM, DN_DIM), F32)
    return pl.pallas_call(
        _gdn_kernel, grid=(nb, nblk),
        in_specs=[pl.BlockSpec(memory_space=pltpu.SMEM),
                  blk(0, False), blk(1, False), blk(2, False), blk(0, True), blk(1, True), blk(2, True),
                  gspec(False), gspec(True)],
        out_specs=[blk(0, False), blk(0, True)],
        out_shape=[out_sd, out_sd],
        scratch_shapes=[state, state],
        compiler_params=_cparams(("parallel", "arbitrary")), name="gdn",
    )(params, qkv, qkv, qkv, qkv, qkv, qkv, gates_t, gates_t)


def _mlstm_kernel(par_ref, qkf_ref, vf_ref, qkb_ref, vb_ref, gf_ref, gb_ref, of_ref, ob_ref,
                  cf_ref, cb_ref, mf_ref, mb_ref):
    i = pl.program_id(1)
    nch = qkf_ref.shape[1] // CHUNK

    @pl.when(i == 0)
    def _():
        for ref in (cf_ref, cb_ref, mf_ref, mb_ref):
            ref[...] = jnp.zeros_like(ref)

    ones_col = _bf(jnp.where(_iota2((CHUNK, ML_V_DIM), 1) == 0, 1.0, 0.0))

    items = []
    scans = []
    for hd in range(ML_HEADS):
        for qk_ref, v_ref, g_ref, o_ref, c_ref, m_ref, ch, rev in (
                (qkf_ref, vf_ref, gf_ref, of_ref, cf_ref, mf_ref, hd, False),
                (qkb_ref, vb_ref, gb_ref, ob_ref, cb_ref, mb_ref, ML_HEADS + hd, True)):
            incl, _, cum = _chunk_masks(rev)
            cum01 = _bf(jnp.where(cum, 1.0, 0.0))
            i_rows = g_ref[0, 4 * DN_HEADS + ch] + par_ref[4 * DN_HEADS + ch]
            f_rows = g_ref[0, 4 * DN_HEADS + 2 * ML_HEADS + ch] + par_ref[4 * DN_HEADS + 2 * ML_HEADS + ch]
            logf = -_softplus(-f_rows)
            bc_rows = _dot_exact_lhs(logf, cum01)
            b_tots = _dot_exact_lhs(logf, jnp.ones((CHUNK, LANES), BF16))
            lw_rows = b_tots[:, :CHUNK] - bc_rows + i_rows
            a_maxs = jnp.max(lw_rows, axis=-1, keepdims=True)
            wgt_rows = jnp.exp(lw_rows - a_maxs)
            run = jnp.concatenate([i_rows - bc_rows, jnp.full((nch, LANES - CHUNK), NEG_BIG, F32)], axis=1)
            live = _iota2((nch, LANES), 1) < CHUNK
            for s in (1, 2, 4, 8, 16, 32):
                run = jnp.where(live, jnp.maximum(run, pltpu.roll(run, (LANES - s) if rev else s, 1)), NEG_BIG)
            dmax_rows = bc_rows + run[:, :CHUNK]
            order = []
            for c in (range(nch - 1, -1, -1) if rev else range(nch)):
                one = slice(c, c + 1)
                it = dict(qk_ref=qk_ref, v_ref=v_ref, o_ref=o_ref, rows=slice(c * CHUNK, (c + 1) * CHUNK),
                          cols=slice(hd * ML_V_DIM, (hd + 1) * ML_V_DIM), incl=incl,
                          bc_row=bc_rows[one], i_row=i_rows[one], wgt_row=wgt_rows[one], b_tot=b_tots[one],
                          a_max=a_maxs[one], dmax_row=dmax_rows[one])
                items.append(it)
                order.append(it)
            scans.append((c_ref, m_ref, hd, order))

    def load_qkv(it):
        qk = it['qk_ref'][0, it['rows'], it['cols']]
        q = qk[:, :ML_QK_DIM]
        kf = qk[:, ML_QK_DIM:].astype(F32) * ML_QK_DIM ** -0.5
        v_ext = jnp.concatenate([it['v_ref'][0, it['rows'], it['cols']], ones_col], axis=1)
        return q, kf, v_ext

    eye_qk = _bf(jnp.where(_iota2((ML_QK_DIM, ML_QK_DIM), 0) == _iota2((ML_QK_DIM, ML_QK_DIM), 1), 1.0, 0.0))

    def bcast(it):
        it['bc_col'], it['dmax_col'] = _col_bcast2(it['bc_row'], it['dmax_row'])

    def transpose_k(it):
        _, kf, _ = load_qkv(it)
        it['k_t'] = _dot_nt(eye_qk, _bf(kf))

    def local(it):
        q, kf, v_ext = load_qkv(it)
        it['cc'] = _dot(_bf(it['k_t'] * it['wgt_row']), v_ext)
        it['qk'] = _dot_nt(q, _bf(kf))

    _each(items, transpose_k)
    _each(items, bcast)
    _each(items, local)

    m_states = [m_ref[hd] for _, m_ref, hd, _ in scans]
    for step in range(nch):
        for n, (_, _, _, order) in enumerate(scans):
            it = order[step]
            it['m_prev'] = m_states[n]
            m_states[n] = jnp.maximum(it['b_tot'] + it['m_prev'], it['a_max'])
            it['m_new'] = m_states[n]
    for n, (_, m_ref, hd, _) in enumerate(scans):
        m_ref[hd] = m_states[n]

    def rescale(it):
        sp = jnp.exp(it['b_tot'] + it['m_prev'] - it['m_new'])
        sc = jnp.exp(it['a_max'] - it['m_new'])
        it['sp'] = jnp.concatenate([sp, sp], axis=1)
        it['cc'] = it['cc'] * jnp.concatenate([sc, sc], axis=1)

    _each(items, rescale)
    c_states = [c_ref[hd] for c_ref, _, hd, _ in scans]
    for step in range(nch):
        for n, (_, _, _, order) in enumerate(scans):
            it = order[step]
            it['c_prev'] = _bf(c_states[n])
            c_states[n] = c_states[n] * it['sp'] + it['cc']
    for n, (c_ref, _, hd, _) in enumerate(scans):
        c_ref[hd] = c_states[n]

    def output(it):
        q, _, v_ext = load_qkv(it)
        inter_log = it['bc_col'] + it['m_prev']
        m_out = jnp.maximum(inter_log, it['dmax_col'])
        d_log = jnp.where(it['incl'], it['bc_col'][:, :CHUNK] - it['bc_row'] + it['i_row'], NEG_BIG)
        s = it['qk'] * jnp.exp(d_log - m_out[:, :CHUNK])
        inter = jnp.exp(inter_log - m_out)
        num = jnp.concatenate([inter, inter], axis=1) * _dot(q, it['c_prev']) + _dot(_bf(s), v_ext)
        den = num[:, ML_V_DIM:ML_V_DIM + 1]
        it['o_ref'][0, it['rows'], it['cols']] = num[:, :ML_V_DIM] / jnp.maximum(jnp.abs(den), jnp.exp(-m_out))

    _each(items, output)


def _mlstm(params, proj3, gates_t):
    nb, ns, _ = proj3.shape
    ts = SEQ_BLOCK
    nblk = ns // ts
    nch = ts // CHUNK
    qk_blk, v_blk = 5, 6

    def at(cblk, reverse):
        if reverse:
            return lambda b, i: (b, nblk - 1 - i, cblk)
        return lambda b, i: (b, i, cblk)

    blk = lambda cblk, reverse: pl.BlockSpec((1, ts, MIX_WIDTH), at(cblk, reverse))
    gspec = lambda reverse: pl.BlockSpec(
        (1, 32, nch, CHUNK), (lambda b, i: (b, 0, nblk - 1 - i, 0)) if reverse else (lambda b, i: (b, 0, i, 0)))
    out_sd = jax.ShapeDtypeStruct((nb, ns, MIX_WIDTH), F32)
    c_state = pltpu.VMEM((ML_HEADS, ML_QK_DIM, 2 * ML_V_DIM), F32)
    m_state = pltpu.VMEM((ML_HEADS, 1, LANES), F32)
    return pl.pallas_call(
        _mlstm_kernel, grid=(nb, nblk),
        in_specs=[pl.BlockSpec(memory_space=pltpu.SMEM),
                  blk(qk_blk, False), blk(v_blk, False), blk(qk_blk, True), blk(v_blk, True),
                  gspec(False), gspec(True)],
        out_specs=[blk(0, False), blk(0, True)],
        out_shape=[out_sd, out_sd],
        scratch_shapes=[c_state, c_state, m_state, m_state],
        compiler_params=_cparams(("parallel", "arbitrary")), name="mlstm",
    )(params, proj3, proj3, proj3, proj3, gates_t, gates_t)


def _pool_kernel(u_ref, up_ref, un_ref, w_ref, sc_ref, o_ref, xs_ref):
    i = pl.program_id(1)
    nblk = pl.num_programs(1)
    ts = u_ref.shape[1]
    _fill_halo(xs_ref, u_ref[0].astype(F32), up_ref[0].astype(F32), un_ref[0].astype(F32), i, nblk)
    pos = i * ts + _iota2((ts, 1), 0)
    seq = nblk * ts
    for g, win in enumerate(POOL_WINDOWS):
        half = win // 2
        cols = slice(g * POOL_DIM, (g + 1) * POOL_DIM)
        acc = jnp.zeros((ts, POOL_DIM), F32)
        for d in range(-half, half):
            acc = acc + xs_ref[8 + d:8 + d + ts, cols]
        cnt = (jnp.minimum(pos + half, seq) - jnp.maximum(pos - half, 0)).astype(F32)
        p = acc / cnt - xs_ref[8:8 + ts, cols]
        y = _dot(p.astype(BF16), w_ref[g]) * sc_ref[:, cols]
        o_ref[0, :, cols] = y.astype(BF16)


def _pool(proj3, pool_w, pool_scale):
    nb, ns, _ = proj3.shape
    ts = SEQ_BLOCK
    cblk = 4
    main, prev, make_next = _halo_specs(ts, MIX_WIDTH, lambda: cblk)
    nxt = make_next(ns // HALO)
    return pl.pallas_call(
        _pool_kernel, grid=(nb, ns // ts),
        in_specs=[pl.BlockSpec((1, ts, MIX_WIDTH), main), pl.BlockSpec((1, HALO, MIX_WIDTH), prev),
                  pl.BlockSpec((1, HALO, MIX_WIDTH), nxt),
                  pl.BlockSpec((len(POOL_WINDOWS), POOL_DIM, POOL_DIM), lambda b, i: (0, 0, 0)),
                  pl.BlockSpec((1, MIX_WIDTH), lambda b, i: (0, 0))],
        out_specs=pl.BlockSpec((1, ts, MIX_WIDTH), lambda b, i: (b, i, 0)),
        out_shape=jax.ShapeDtypeStruct((nb, ns, MIX_WIDTH), BF16),
        scratch_shapes=[pltpu.VMEM((ts + 16, MIX_WIDTH), F32)],
        compiler_params=_cparams(("parallel", "parallel")), name="pool",
    )(proj3, proj3, proj3, pool_w, pool_scale)


def _merge_kernel(x_ref, h_ref, of_ref, ob_ref, z_ref, yp_ref, mf_ref, mb_ref, mo_ref,
                  wg_ref, wb_ref, wo_ref, dnw_ref, mlw_ref, lg_ref, lb_ref, o_ref, obf_ref):
    o_dn = of_ref[...] + ob_ref[...]
    h_ml = mf_ref[...] + mb_ref[...]
    dn_parts = []
    ml_parts = []
    for hd in range(DN_HEADS):
        cols = slice(hd * DN_DIM, (hd + 1) * DN_DIM)
        a = o_dn[:, cols]
        dn_parts.append(a * lax.rsqrt(jnp.mean(a * a, -1, keepdims=True) + NORM_EPS))
        c = h_ml[:, cols]
        c = c - jnp.mean(c, -1, keepdims=True)
        ml_parts.append(c * lax.rsqrt(jnp.mean(c * c, -1, keepdims=True) + NORM_EPS))
    z = z_ref[...].astype(F32)
    y_dn = jnp.concatenate(dn_parts, axis=1) * dnw_ref[...] * (z * _sigmoid(z))
    y_ml = jnp.concatenate(ml_parts, axis=1) * mlw_ref[...] * _sigmoid(mo_ref[...].astype(F32))
    branches = (y_dn.astype(BF16), yp_ref[...], y_ml.astype(BF16))
    x = x_ref[...]
    merged = jnp.zeros(o_ref.shape, F32)
    for r in range(N_BRANCH):
        gate = _sigmoid(_dot(x, wg_ref[:, r * D_MODEL:(r + 1) * D_MODEL]))
        merged = merged + gate * _dot(branches[r], wb_ref[r])
    mix = _dot(merged.astype(BF16), wo_ref[...])
    y = _ln_rows(ALPHA * h_ref[...] + mix, lg_ref[...], lb_ref[...])
    o_ref[...] = y
    obf_ref[...] = y.astype(BF16)


def _merge(x_bf, h, o_f, o_b, proj, y_pool, m_f, m_b, w_gate, w_branch, w_out, dn_norm_w, ml_norm_w, ln_g, ln_b):
    t, d = h.shape
    tm = ROW_BLOCK
    row = lambda width, cblk=0: pl.BlockSpec((tm, width), lambda i: (i, cblk))
    const = lambda shape: pl.BlockSpec(shape, lambda i: (0,) * len(shape), pipeline_mode=pl.Buffered(1))
    return pl.pallas_call(
        _merge_kernel, grid=(t // tm,),
        in_specs=[row(d), row(d), row(MIX_WIDTH), row(MIX_WIDTH), row(MIX_WIDTH, 3), row(MIX_WIDTH),
                  row(MIX_WIDTH), row(MIX_WIDTH), row(MIX_WIDTH, 7),
                  const((d, N_BRANCH * d)), const((N_BRANCH, MIX_WIDTH, d)), const((d, d)),
                  const((1, MIX_WIDTH)), const((1, MIX_WIDTH)), const((1, d)), const((1, d))],
        out_specs=[row(d), row(d)],
        out_shape=[jax.ShapeDtypeStruct((t, d), F32), jax.ShapeDtypeStruct((t, d), BF16)],
        compiler_params=_cparams(("parallel",)), name="merge",
    )(x_bf, h, o_f, o_b, proj, y_pool, m_f, m_b, proj, w_gate, w_branch, w_out,
      dn_norm_w.reshape(1, -1), ml_norm_w.reshape(1, -1), ln_g.reshape(1, -1), ln_b.reshape(1, -1))


def _route_kernel(h_ref, rw_ref, rb_ref, e_ref, gw_ref, rank_ref, cnt_ref, base_ref):
    i = pl.program_id(0)
    tm = h_ref.shape[0]
    G, J = N_GROUPS, EXPERTS_PER_GROUP

    @pl.when(i == 0)
    def _():
        base_ref[...] = jnp.zeros_like(base_ref)

    hv = h_ref[...]
    h_hi = _bf(hv)
    h_lo = _bf(hv - h_hi.astype(F32))
    rw = rw_ref[...]
    rw_hi = _bf(rw)
    rw_lo = _bf(rw - rw_hi.astype(F32))
    pre = _dot(h_hi, rw_hi) + (_dot(h_lo, rw_hi) + _dot(h_hi, rw_lo))
    sel_rows = jnp.where(_iota2((N_EXPERTS, LANES), 0) == _iota2((N_EXPERTS, LANES), 1), 1.0, 0.0).astype(BF16)
    p1, p2, p3 = _split3(pre)
    pre_t = _dot_nt(sel_rows, p1) + (_dot_nt(sel_rows, p2) + _dot_nt(sel_rows, p3))
    score = [_sigmoid(pre_t[j * G:(j + 1) * G, :]) for j in range(J)]
    sel = [score[j] + rb_ref[j][:, :1] for j in range(J)]
    pair = None
    for a in range(J):
        for b in range(a + 1, J):
            s = sel[a] + sel[b]
            pair = s if pair is None else jnp.maximum(pair, s)
    gid = _iota2((G, tm), 0)
    gmax = jnp.max(pair, axis=0, keepdims=True)
    grp = jnp.min(jnp.where(pair == gmax, gid.astype(F32), float(G)), axis=0, keepdims=True).astype(jnp.int32)
    in_grp = gid == grp
    pick = lambda arr: jnp.sum(jnp.where(in_grp, arr, 0.0), axis=0, keepdims=True)
    v = [pick(sel[j]) for j in range(J)]
    u = [pick(score[j]) for j in range(J)]
    rank = []
    for j in range(J):
        rj = jnp.zeros((1, tm), jnp.int32)
        for a in range(J):
            if a == j:
                continue
            ahead = (v[a] >= v[j]) if a < j else (v[a] > v[j])
            rj = rj + ahead.astype(jnp.int32)
        rank.append(rj)
    loc = []
    wsel = []
    for kk in range(TOP_K):
        lk = jnp.zeros((1, tm), jnp.int32)
        wk = jnp.zeros((1, tm), F32)
        for j in range(J):
            hit = rank[j] == kk
            lk = lk + jnp.where(hit, j, 0)
            wk = wk + jnp.where(hit, u[j], 0.0)
        loc.append(lk)
        wsel.append(wk)
    wsum = wsel[0] + wsel[1]
    e_ref[...] = jnp.concatenate([grp * J + loc[0], grp * J + loc[1]], axis=0)
    gw_ref[...] = jnp.concatenate([wsel[0] / wsum, wsel[1] / wsum], axis=0)
    before = jnp.where(_iota2((tm, tm), 0) < _iota2((tm, tm), 1), 1.0, 0.0).astype(BF16)
    ones = jnp.ones((tm, LANES), BF16)
    r_k = [jnp.zeros((1, tm), F32) for _ in range(TOP_K)]
    for j in range(J):
        hits = [in_grp & (loc[kk] == j) for kk in range(TOP_K)]
        oh = jnp.where(hits[0] | hits[1], 1.0, 0.0)
        pos = _dot(oh.astype(BF16), before) + base_ref[j][:, :1]
        for kk in range(TOP_K):
            r_k[kk] = r_k[kk] + jnp.sum(jnp.where(hits[kk], pos, 0.0), axis=0, keepdims=True)
        base_ref[j] = base_ref[j] + _dot(oh.astype(BF16), ones)
    rank_ref[...] = jnp.concatenate(r_k, axis=0).astype(jnp.int32)
    cnt_ref[...] = base_ref[...]


def _route(h, router_w, router_b):
    t, d = h.shape
    tm = SEQ_BLOCK
    G, J = N_GROUPS, EXPERTS_PER_GROUP
    perm = np.asarray([g * J + j for j in range(J) for g in range(G)])
    rw = jnp.zeros((d, LANES), F32).at[:, :N_EXPERTS].set(router_w.astype(F32)[:, perm])
    rb = jnp.broadcast_to(router_b.astype(F32)[perm].reshape(J, G, 1), (J, G, LANES))
    pair = lambda dt: jax.ShapeDtypeStruct((TOP_K, t), dt)
    tok = pl.BlockSpec((TOP_K, tm), lambda i: (0, i))
    return pl.pallas_call(
        _route_kernel, grid=(t // tm,),
        in_specs=[pl.BlockSpec((tm, d), lambda i: (i, 0)), pl.BlockSpec((d, LANES), lambda i: (0, 0)),
                  pl.BlockSpec((J, G, LANES), lambda i: (0, 0, 0))],
        out_specs=[tok, tok, tok, pl.BlockSpec((J, G, LANES), lambda i: (0, 0, 0))],
        out_shape=[pair(jnp.int32), pair(F32), pair(jnp.int32), jax.ShapeDtypeStruct((J, G, LANES), F32)],
        scratch_shapes=[pltpu.VMEM((J, G, LANES), F32)],
        compiler_params=_cparams(("arbitrary",)), name="route",
    )(h, rw, rb)


def _dispatch_kernel(pend_ref, dest_ref, h_ref, xb_ref, zero_ref, sem, zsem):
    tm = h_ref.shape[0]

    @pl.when(pl.program_id(0) == 0)
    def _():
        zero_ref[...] = jnp.zeros_like(zero_ref)

        def fill(start):
            if not isinstance(start, int):
                start = pl.multiple_of(start, MOE_ROWS)
            return pltpu.make_async_copy(zero_ref, xb_ref.at[pl.ds(start, MOE_ROWS)], zsem)

        n_rows = xb_ref.shape[0]
        jobs = []
        for e in range(N_EXPERTS):
            jobs.append((pend_ref[e] > (pend_ref[e - 1] if e > 0 else 0), pend_ref[e] - MOE_ROWS))
            tail = n_rows - (e + 1) * MOE_ROWS
            jobs.append((tail >= pend_ref[N_EXPERTS - 1], tail))
        for cond, start in jobs:
            pl.when(cond)(lambda start=start: fill(start).start())
        for cond, start in jobs:
            pl.when(cond)(lambda start=start: fill(start).wait())

    def row_copy(t, kk):
        return pltpu.make_async_copy(h_ref.at[pl.ds(t, 1)], xb_ref.at[pl.ds(dest_ref[kk, t], 1)], sem)

    def start(t, carry):
        for kk in range(TOP_K):
            row_copy(t, kk).start(priority=kk % 2)
        return carry

    def wait(t, carry):
        for kk in range(TOP_K):
            row_copy(t, kk).wait()
        return carry

    lax.fori_loop(0, tm, start, 0, unroll=8)
    lax.fori_loop(0, tm, wait, 0, unroll=8)


def _dispatch(pend, dest, h, n_rows):
    t, d = h.shape
    tm = DISPATCH_BLOCK
    grid_spec = pltpu.PrefetchScalarGridSpec(
        num_scalar_prefetch=1, grid=(t // tm,),
        in_specs=[pl.BlockSpec((TOP_K, tm), lambda i, pe: (0, i), memory_space=pltpu.SMEM),
                  pl.BlockSpec((tm, d), lambda i, pe: (i, 0))],
        out_specs=pl.BlockSpec(memory_space=pl.ANY),
        scratch_shapes=[pltpu.VMEM((MOE_ROWS, d), F32), pltpu.SemaphoreType.DMA(()), pltpu.SemaphoreType.DMA(())])
    return pl.pallas_call(
        _dispatch_kernel, grid_spec=grid_spec, out_shape=jax.ShapeDtypeStruct((n_rows, d), F32),
        compiler_params=_cparams(("arbitrary",)), name="moe_dispatch",
    )(pend, dest, h)


def _expert_kernel(be_ref, nact_ref, x_ref, w1_ref, w3_ref, w2_ref, o_ref):
    i = pl.program_id(0)

    @pl.when(i < nact_ref[0])
    def _():
        x = x_ref[...].astype(BF16)
        a = _dot(x, w1_ref[0])
        b = _dot(x, w3_ref[0])
        act = a * _sigmoid(a) * b
        o_ref[...] = _dot(act.astype(BF16), w2_ref[0])

    @pl.when(i >= nact_ref[0])
    def _():
        o_ref[...] = jnp.zeros_like(o_ref)


def _experts(block_expert, n_active, xb, w1, w3, w2):
    n_rows, d = xb.shape
    bm = MOE_ROWS
    grid_spec = pltpu.PrefetchScalarGridSpec(
        num_scalar_prefetch=2, grid=(n_rows // bm,),
        in_specs=[pl.BlockSpec((bm, d), lambda i, be, na: (jnp.minimum(i, na[0] - 1), 0)),
                  pl.BlockSpec((1, d, D_EXPERT), lambda i, be, na: (be[i], 0, 0)),
                  pl.BlockSpec((1, d, D_EXPERT), lambda i, be, na: (be[i], 0, 0)),
                  pl.BlockSpec((1, D_EXPERT, d), lambda i, be, na: (be[i], 0, 0))],
        out_specs=pl.BlockSpec((bm, d), lambda i, be, na: (i, 0)))
    return pl.pallas_call(
        _expert_kernel, grid_spec=grid_spec, out_shape=jax.ShapeDtypeStruct((n_rows, d), F32),
        compiler_params=_cparams(("arbitrary",)), name="moe_experts",
    )(block_expert, n_active, xb, w1, w3, w2)


def _combine_kernel(dest_ref, dest_next_ref, h_ref, gw_ref, yb_ref, lg_ref, lb_ref, o_ref, obf_ref, buf_ref, sem):
    i = pl.program_id(0)
    tm = h_ref.shape[0]
    slot = lax.rem(i, 2)

    def gather(idx_ref, s):
        def start(t, carry):
            for kk in range(TOP_K):
                pltpu.make_async_copy(yb_ref.at[pl.ds(idx_ref[kk, t], 1)], buf_ref.at[s, kk, pl.ds(t, 1)],
                                      sem.at[s]).start(priority=kk % 2)
            return carry
        lax.fori_loop(0, tm, start, 0, unroll=8)

    @pl.when(i == 0)
    def _():
        gather(dest_ref, 0)

    @pl.when(i + 1 < pl.num_programs(0))
    def _():
        gather(dest_next_ref, 1 - slot)

    def wait(t, carry):
        for kk in range(TOP_K):
            pltpu.make_async_copy(yb_ref.at[pl.ds(0, 1)], buf_ref.at[slot, kk, pl.ds(t, 1)], sem.at[slot]).wait()
        return carry

    lax.fori_loop(0, tm, wait, 0, unroll=8)
    gw = gw_ref[...]
    ffn = gw[:, 0:1] * buf_ref[slot, 0] + gw[:, 1:2] * buf_ref[slot, 1]
    y = _ln_rows(ALPHA * h_ref[...] + ffn, lg_ref[...], lb_ref[...])
    o_ref[...] = y
    obf_ref[...] = y.astype(BF16)


def _combine(dest, h, gw_cols, yb, ln_g, ln_b):
    t, d = h.shape
    tm = COMBINE_BLOCK
    row = pl.BlockSpec((tm, d), lambda i: (i, 0))
    vec = pl.BlockSpec((1, d), lambda i: (0, 0))
    nsteps = t // tm
    return pl.pallas_call(
        _combine_kernel, grid=(nsteps,),
        in_specs=[pl.BlockSpec((TOP_K, tm), lambda i: (0, i), memory_space=pltpu.SMEM),
                  pl.BlockSpec((TOP_K, tm), lambda i: (0, jnp.minimum(i + 1, nsteps - 1)), memory_space=pltpu.SMEM),
                  row, pl.BlockSpec((tm, TOP_K), lambda i: (i, 0)), pl.BlockSpec(memory_space=pl.ANY), vec, vec],
        out_specs=[row, row],
        out_shape=[jax.ShapeDtypeStruct((t, d), F32), jax.ShapeDtypeStruct((t, d), BF16)],
        scratch_shapes=[pltpu.VMEM((2, TOP_K, tm, d), F32), pltpu.SemaphoreType.DMA((2,))],
        compiler_params=_cparams(("arbitrary",)), name="moe_combine",
    )(dest, dest, h, gw_cols, yb, ln_g.reshape(1, d), ln_b.reshape(1, d))


def _moe(h, router_w, router_b, w1, w3, w2, ln_g, ln_b):
    t, d = h.shape
    expert, gate_w, rank, counts = _route(h, router_w, router_b)
    cnt = counts[:, :, 0].T.reshape(N_EXPERTS).astype(jnp.int32)
    padded = ((cnt + MOE_ROWS - 1) // MOE_ROWS) * MOE_ROWS
    pend = jnp.cumsum(padded)
    pstart = pend - padded
    is_e = expert[..., None] == jnp.arange(N_EXPERTS, dtype=jnp.int32)
    dest = jnp.sum(jnp.where(is_e, pstart.astype(jnp.int32), 0), axis=-1) + rank
    n_rows = t * TOP_K + N_EXPERTS * MOE_ROWS
    n_blocks = n_rows // MOE_ROWS
    block_start = jnp.arange(n_blocks, dtype=jnp.int32) * MOE_ROWS
    block_expert = jnp.minimum(jnp.sum((pend[None, :] <= block_start[:, None]).astype(jnp.int32), axis=1), N_EXPERTS - 1)
    n_active = (pend[-1:] // MOE_ROWS).astype(jnp.int32)
    xb = _dispatch(pend.astype(jnp.int32), dest, h, n_rows)
    yb = _experts(block_expert, n_active, xb, w1, w3, w2)
    return _combine(dest, h, gate_w.T, yb, ln_g, ln_b)


def kernel(x, ln_in_g, ln_in_b, w_in, dn_conv_w, dn_a_log, dn_dt_bias, dn_norm_w, pool_w, pool_scale,
           ml_gate_b, ml_norm_w, w_branch, w_out, ln_mix_g, ln_mix_b, router_w, router_b,
           moe_w1, moe_w3, moe_w2, ln_ffn_g, ln_ffn_b):
    nb, ns, d = x.shape
    t = nb * ns
    h, h_bf = _layer_norm_in(x.reshape(t, d), ln_in_g, ln_in_b)
    for l in range(DEPTH):
        w_main, w_small, w_gate = _split_in_proj(w_in[l])
        proj = _matmul(h_bf, w_main, BF16, 512, 1024, "in_proj")
        small = _matmul(h_bf, w_small, F32, 512, LANES, "in_proj_gates")
        proj3 = proj.reshape(nb, ns, MAIN_WIDTH)
        gates_t = small[:, :32].reshape(nb, ns // CHUNK, CHUNK, 32).transpose(0, 3, 1, 2)
        params = jnp.concatenate([dn_a_log[l], dn_dt_bias[l], ml_gate_b[l]]).astype(F32)
        qkv = _dn_prep(proj3, dn_conv_w[l].astype(F32))
        o_f, o_b = _gdn(params, qkv, gates_t)
        y_pool = _pool(proj3, pool_w[l].astype(BF16), pool_scale[l].astype(F32).reshape(1, -1))
        m_f, m_b = _mlstm(params, proj3, gates_t)
        flat = lambda a: a.reshape(t, -1)
        h, h_bf = _merge(h_bf, h, flat(o_f), flat(o_b), proj, flat(y_pool), flat(m_f), flat(m_b),
                         w_gate, w_branch[l].astype(BF16), w_out[l].astype(BF16),
                         dn_norm_w[l].astype(F32), ml_norm_w[l].astype(F32), ln_mix_g[l], ln_mix_b[l])
        h, h_bf = _moe(h, router_w, router_b, moe_w1[l].astype(BF16), moe_w3[l].astype(BF16),
                       moe_w2[l].astype(BF16), ln_ffn_g[l], ln_ffn_b[l])
    return h.reshape(nb, ns, d)
```

```python
import jax
import jax.numpy as jnp
import numpy as np
from jax import lax
from jax.experimental import pallas as pl
from jax.experimental.pallas import tpu as pltpu

F32 = jnp.float32
BF16 = jnp.bfloat16

D_MODEL = 1024
DEPTH = 2
MIX_WIDTH = D_MODEL // 2
DN_HEADS = 4
DN_DIM = MIX_WIDTH // DN_HEADS
DN_CONV = 5
CHUNK = 64
POOL_WINDOWS = (2, 4, 8, 16)
POOL_DIM = MIX_WIDTH // len(POOL_WINDOWS)
ML_HEADS = 4
ML_V_DIM = MIX_WIDTH // ML_HEADS
ML_QK_DIM = ML_V_DIM // 2
N_BRANCH = 3
N_EXPERTS = 32
N_GROUPS = 8
EXPERTS_PER_GROUP = N_EXPERTS // N_GROUPS
TOP_K = 2
D_EXPERT = D_MODEL // 2
LN_EPS = 1e-5
NORM_EPS = 1e-6
ALPHA = (2.0 * DEPTH) ** 0.25

LANES = 128
HALO = 16
SEQ_BLOCK = 512
ROW_BLOCK = 512
MOE_ROWS = 256
DISPATCH_BLOCK = 256
COMBINE_BLOCK = 128
GDN_HEAD_GROUP = 4
NEG_BIG = -1e30
VMEM_LIMIT = 56 * 1024 * 1024

_SPLITS = (3 * MIX_WIDTH, MIX_WIDTH, 2 * DN_HEADS, 2 * DN_HEADS, MIX_WIDTH, ML_HEADS * ML_QK_DIM,
           ML_HEADS * ML_QK_DIM, MIX_WIDTH, MIX_WIDTH, 2 * ML_HEADS, 2 * ML_HEADS, N_BRANCH * D_MODEL)
_OFF = [0] + [int(c) for c in np.cumsum(_SPLITS)]
MAIN_WIDTH = 8 * MIX_WIDTH


def _split_in_proj(w):
    o = _OFF
    w = w.astype(BF16)
    parts = [w[:, o[0]:o[2]], w[:, o[4]:o[5]]]
    for h in range(ML_HEADS):
        parts.append(w[:, o[5] + h * ML_QK_DIM:o[5] + (h + 1) * ML_QK_DIM])
        parts.append(w[:, o[6] + h * ML_QK_DIM:o[6] + (h + 1) * ML_QK_DIM])
    parts.append(w[:, o[7]:o[9]])
    main = jnp.concatenate(parts, axis=1)
    n_small = (o[4] - o[2]) + (o[11] - o[9])
    small = jnp.concatenate([w[:, o[2]:o[4]], w[:, o[9]:o[11]], jnp.zeros((w.shape[0], LANES - n_small), BF16)], axis=1)
    return main, small, w[:, o[11]:o[12]]


def _cparams(sem, vmem=VMEM_LIMIT):
    return pltpu.CompilerParams(dimension_semantics=sem, vmem_limit_bytes=vmem)


def _dot(a, b):
    return jnp.dot(a, b, preferred_element_type=F32)


def _dot_nt(a, b):
    return lax.dot_general(a, b, (((1,), (1,)), ((), ())), preferred_element_type=F32)


def _dot_tn(a, b):
    return lax.dot_general(a, b, (((0,), (0,)), ((), ())), preferred_element_type=F32)


def _split3(a):
    a1 = a.astype(BF16)
    r1 = a - a1.astype(F32)
    a2 = r1.astype(BF16)
    a3 = (r1 - a2.astype(F32)).astype(BF16)
    return a1, a2, a3


def _dot_exact_lhs(a, b01):
    a1, a2, a3 = _split3(a)
    return _dot(a1, b01) + (_dot(a2, b01) + _dot(a3, b01))


def _sigmoid(x):
    return 1.0 / (1.0 + jnp.exp(-x))


def _softplus(x):
    return jnp.maximum(x, 0.0) + jnp.log(1.0 + jnp.exp(-jnp.abs(x)))


def _iota2(shape, dim):
    return lax.broadcasted_iota(jnp.int32, shape, dim)


def _ln_rows(x, g, b):
    mu = jnp.mean(x, -1, keepdims=True)
    xc = x - mu
    var = jnp.mean(xc * xc, -1, keepdims=True)
    return xc * lax.rsqrt(var + LN_EPS) * g + b


def _ln_kernel(x_ref, g_ref, b_ref, o_ref, obf_ref):
    y = _ln_rows(x_ref[...], g_ref[...], b_ref[...])
    o_ref[...] = y
    obf_ref[...] = y.astype(BF16)


def _layer_norm_in(x2, g, b):
    t, d = x2.shape
    tm = ROW_BLOCK
    row = pl.BlockSpec((tm, d), lambda i: (i, 0))
    vec = pl.BlockSpec((1, d), lambda i: (0, 0))
    return pl.pallas_call(
        _ln_kernel, grid=(t // tm,), in_specs=[row, vec, vec], out_specs=[row, row],
        out_shape=[jax.ShapeDtypeStruct((t, d), F32), jax.ShapeDtypeStruct((t, d), BF16)],
        compiler_params=_cparams(("parallel",)), name="ln_in",
    )(x2, g.reshape(1, d), b.reshape(1, d))


def _matmul_kernel(x_ref, w_ref, o_ref):
    o_ref[...] = _dot(x_ref[...], w_ref[...]).astype(o_ref.dtype)


def _matmul(x, w, out_dtype, tm, tn, name):
    t, k = x.shape
    n = w.shape[1]
    return pl.pallas_call(
        _matmul_kernel, grid=(n // tn, t // tm),
        in_specs=[pl.BlockSpec((tm, k), lambda j, i: (i, 0)), pl.BlockSpec((k, tn), lambda j, i: (0, j))],
        out_specs=pl.BlockSpec((tm, tn), lambda j, i: (i, j)),
        out_shape=jax.ShapeDtypeStruct((t, n), out_dtype),
        compiler_params=_cparams(("parallel", "parallel")), name=name,
    )(x, w)


def _fill_halo(xs_ref, main, prev, nxt, i, nblk):
    ts = main.shape[0]
    xs_ref[0:8, :] = jnp.where(i > 0, prev[HALO - 8:HALO, :], 0.0)
    xs_ref[8:8 + ts, :] = main
    xs_ref[8 + ts:16 + ts, :] = jnp.where(i < nblk - 1, nxt[0:8, :], 0.0)


def _dn_prep_kernel(x_ref, xp_ref, xn_ref, cw_ref, o_ref, xs_ref):
    i = pl.program_id(1)
    j = pl.program_id(2)
    ts = x_ref.shape[1]
    _fill_halo(xs_ref, x_ref[0].astype(F32), xp_ref[0].astype(F32), xn_ref[0].astype(F32), i, pl.num_programs(1))
    pad = DN_CONV // 2
    for hd in range(DN_HEADS):
        cols = slice(hd * DN_DIM, (hd + 1) * DN_DIM)
        acc = jnp.zeros((ts, DN_DIM), F32)
        for w in range(DN_CONV):
            acc = acc + xs_ref[8 + w - pad:8 + w - pad + ts, cols] * cw_ref[w:w + 1, cols]
        y = acc * _sigmoid(acc)
        inv = lax.rsqrt(jnp.sum(y * y, -1, keepdims=True) + NORM_EPS)
        scale = jnp.where(j == 0, inv * DN_DIM ** -0.5, jnp.where(j == 1, inv, 1.0))
        o_ref[0, :, cols] = (y * scale).astype(BF16)


def _halo_specs(ts, width, col_of):
    r = ts // HALO

    def main(b, i, *rest):
        return (b, i, col_of(*rest))

    def prev(b, i, *rest):
        return (b, jnp.maximum(i * r - 1, 0), col_of(*rest))

    def make_next(nrow_blocks):
        def nxt(b, i, *rest):
            return (b, jnp.minimum((i + 1) * r, nrow_blocks - 1), col_of(*rest))
        return nxt

    return main, prev, make_next


def _dn_prep(proj3, conv_w):
    nb, ns, _ = proj3.shape
    ts = SEQ_BLOCK
    main, prev, make_next = _halo_specs(ts, MIX_WIDTH, lambda j: j)
    nxt = make_next(ns // HALO)
    return pl.pallas_call(
        _dn_prep_kernel, grid=(nb, ns // ts, 3),
        in_specs=[pl.BlockSpec((1, ts, MIX_WIDTH), main), pl.BlockSpec((1, HALO, MIX_WIDTH), prev),
                  pl.BlockSpec((1, HALO, MIX_WIDTH), nxt), pl.BlockSpec((DN_CONV, MIX_WIDTH), lambda b, i, j: (0, j))],
        out_specs=pl.BlockSpec((1, ts, MIX_WIDTH), main),
        out_shape=jax.ShapeDtypeStruct((nb, ns, 3 * MIX_WIDTH), BF16),
        scratch_shapes=[pltpu.VMEM((ts + 16, MIX_WIDTH), F32)],
        compiler_params=_cparams(("parallel", "parallel", "parallel")), name="dn_prep",
    )(proj3, proj3, proj3, conv_w)


def _chunk_masks(rev):
    r = _iota2((CHUNK, CHUNK), 0)
    c = _iota2((CHUNK, CHUNK), 1)
    if rev:
        return r <= c, r < c, r >= c
    return r >= c, r > c, r <= c


def _col_bcast2(row_a, row_b):
    r = _iota2((CHUNK, CHUNK), 0)
    c = _iota2((CHUNK, CHUNK), 1)
    eye = r == c
    diag = jnp.concatenate([jnp.where(eye, jnp.broadcast_to(row_a, (CHUNK, CHUNK)), 0.0),
                            jnp.where(eye, jnp.broadcast_to(row_b, (CHUNK, CHUNK)), 0.0)], axis=1)
    rr = _iota2((2 * CHUNK, 2 * LANES), 0)
    cc = _iota2((2 * CHUNK, 2 * LANES), 1)
    ones_bd = jnp.where((rr // CHUNK) == (cc // LANES), 1.0, 0.0).astype(BF16)
    hi = _bf(diag)
    lo = _bf(diag - hi.astype(F32))
    out = _dot(hi, ones_bd) + _dot(lo, ones_bd)
    return out[:, :LANES], out[:, LANES:]


def _bf(x):
    return x.astype(BF16)


def _each(items, fn):
    for it in items:
        fn(it)


def _tri_inverse_stages():
    r = _iota2((CHUNK, CHUNK), 0)
    c = _iota2((CHUNK, CHUNK), 1)
    same16 = (r // 16) == (c // 16)
    same32 = (r // 32) == (c // 32)

    def start(it):
        it['md'] = jnp.where(same16, it['m'], 0.0)
        md_bf = _bf(it['md'])
        it['n'] = -it['md']
        it['a'] = _dot(md_bf, md_bf)

    def square(it):
        a_bf = _bf(it['a'])
        both = _dot(jnp.concatenate([_bf(it['n']), a_bf], axis=0), a_bf)
        it['n'] = it['n'] + it['a'] + both[:CHUNK]
        it['a'] = both[CHUNK:]

    def last_power(it):
        it['n'] = it['n'] + it['a'] + _dot(_bf(it['n']), _bf(it['a']))

    def eliminate_a(mask_of):
        def fn(it):
            cpl = mask_of(it)
            it['x'] = cpl + _dot(_bf(it['n']), _bf(cpl))
        return fn

    def eliminate_b(it):
        it['n'] = it['n'] - it['x'] - _dot(_bf(it['x']), _bf(it['n']))

    return [start, square, square, last_power,
            eliminate_a(lambda it: jnp.where(same32, it['m'] - it['md'], 0.0)), eliminate_b,
            eliminate_a(lambda it: jnp.where(same32, 0.0, it['m'])), eliminate_b]


def _gdn_kernel(par_ref, qf_ref, kf_ref, vf_ref, qb_ref, kb_ref, vb_ref, gf_ref, gb_ref,
                of_ref, ob_ref, sf_ref, sb_ref):
    i = pl.program_id(1)
    nch = qf_ref.shape[1] // CHUNK

    @pl.when(i == 0)
    def _():
        sf_ref[...] = jnp.zeros_like(sf_ref)
        sb_ref[...] = jnp.zeros_like(sb_ref)

    def build(hd):
        cols = slice(hd * DN_DIM, (hd + 1) * DN_DIM)
        items = []
        scans = []
        for q_ref, k_ref, v_ref, g_ref, o_ref, s_ref, ch, rev in (
                (qf_ref, kf_ref, vf_ref, gf_ref, of_ref, sf_ref, hd, False),
                (qb_ref, kb_ref, vb_ref, gb_ref, ob_ref, sb_ref, DN_HEADS + hd, True)):
            incl, strict, cum = _chunk_masks(rev)
            cum01 = _bf(jnp.where(cum, 1.0, 0.0))
            a_rows = g_ref[0, ch]
            b_rows = g_ref[0, 2 * DN_HEADS + ch]
            g_rows = (-jnp.exp(jnp.full((nch, CHUNK), par_ref[ch], F32))
                      * _softplus(a_rows + par_ref[2 * DN_HEADS + ch]))
            beta_rows = _sigmoid(b_rows)
            gc_rows = _dot_exact_lhs(g_rows, cum01)
            g_tots = _dot_exact_lhs(g_rows, jnp.ones((CHUNK, LANES), BF16))
            order = []
            for c in (range(nch - 1, -1, -1) if rev else range(nch)):
                rows = slice(c * CHUNK, (c + 1) * CHUNK)
                it = dict(q_ref=q_ref, k_ref=k_ref, v_ref=v_ref, o_ref=o_ref, rows=rows, cols=cols, incl=incl,
                          strict=strict, gc_row=gc_rows[c:c + 1], beta_row=beta_rows[c:c + 1], g_tot=g_tots[c:c + 1])
                items.append(it)
                order.append(it)
            scans.append(dict(s_ref=s_ref, hd=hd, order=order))
        return items, scans

    def bcast(it):
        it['gc_col'], it['beta_col'] = _col_bcast2(it['gc_row'], it['beta_row'])

    def gram(it):
        q = it['q_ref'][0, it['rows'], it['cols']]
        k = it['k_ref'][0, it['rows'], it['cols']]
        decay = jnp.exp(jnp.where(it['incl'], it['gc_col'][:, :CHUNK] - it['gc_row'], NEG_BIG))
        kb = k.astype(F32) * it['beta_col']
        both = _dot_nt(jnp.concatenate([_bf(kb), q], axis=0), k)
        it['m'] = jnp.where(it['strict'], both[:CHUNK] * decay, 0.0)
        it['attn'] = _bf(both[CHUNK:] * decay)

    def solve(it):
        q = it['q_ref'][0, it['rows'], it['cols']]
        kf = it['k_ref'][0, it['rows'], it['cols']].astype(F32)
        v = it['v_ref'][0, it['rows'], it['cols']]
        e_col = jnp.exp(it['gc_col'])
        rhs = jnp.concatenate([v.astype(F32) * it['beta_col'], kf * it['beta_col'] * e_col], axis=1)
        sol = rhs + _dot(_bf(it['n']), _bf(rhs))
        it['u'] = sol[:, :DN_DIM]
        it['wq'] = jnp.concatenate([_bf(sol[:, DN_DIM:]), _bf(q.astype(F32) * e_col)], axis=0)
        it['kdec'] = _bf(kf * jnp.exp(it['g_tot'] - it['gc_col']))
        it['e_tot'] = jnp.exp(it['g_tot'])

    stages =[bcast, gram] + _tri_inverse_stages() + [solve]
    assert len(stages) >= nch

    def scan_step(scans, step):
        for sc in scans:
            if step == 0:
                sc['state'] = sc['s_ref'][sc['hd']]
            it = sc['order'][step]
            sc['ws'] = _dot(it['wq'], _bf(sc['state']))
        for sc in scans:
            it = sc['order'][step]
            sc['v_new'] = _bf(it['u'] - sc['ws'][:CHUNK])
            sc['state'] = sc['state'] * it['e_tot'] + _dot_tn(it['kdec'], sc['v_new'])
        for sc in scans:
            it = sc['order'][step]
            it['o_ref'][0, it['rows'], it['cols']] = sc['ws'][CHUNK:] + _dot(it['attn'], sc['v_new'])
            if step == nch - 1:
                sc['s_ref'][sc['hd']] = sc['state']

    pending = None
    for first in range(0, DN_HEADS, GDN_HEAD_GROUP):
        items, scans = [], []
        for hd in range(first, first + GDN_HEAD_GROUP):
            head_items, head_scans = build(hd)
            items += head_items
            scans += head_scans
        for n_stage, stage in enumerate(stages):
            _each(items, stage)
            if pending is not None and n_stage < nch:
                scan_step(pending, n_stage)
        pending = scans
    for step in range(nch):
        scan_step(pending, step)


def _gdn(params, qkv, gates_t):
    nb, ns, _ = qkv.shape
    ts = SEQ_BLOCK
    nblk = ns // ts
    nch = ts // CHUNK

    def at(cblk, reverse):
        if reverse:
            return lambda b, i: (b, nblk - 1 - i, cblk)
        return lambda b, i: (b, i, cblk)

    blk = lambda cblk, reverse: pl.BlockSpec((1, ts, MIX_WIDTH), at(cblk, reverse))
    gspec = lambda reverse: pl.BlockSpec(
        (1, 32, nch, CHUNK), (lambda b, i: (b, 0, nblk - 1 - i, 0)) if reverse else (lambda b, i: (b, 0, i, 0)))
    out_sd = jax.ShapeDtypeStruct((nb, ns, MIX_WIDTH), F32)
    state = pltpu.VMEM((DN_HEADS, DN_DIM, DN_DIM), F32)
    return pl.pallas_call(
        _gdn_kernel, grid=(nb, nblk),
        in_specs=[pl.BlockSpec(memory_space=pltpu.SMEM),
                  blk(0, False), blk(1, False), blk(2, False), blk(0, True), blk(1, True), blk(2, True),
                  gspec(False), gspec(True)],
        out_specs=[blk(0, False), blk(0, True)],
        out_shape=[out_sd, out_sd],
        scratch_shapes=[state, state],
        compiler_params=_cparams(("parallel", "arbitrary")), name="gdn",
    )(params, qkv, qkv, qkv, qkv, qkv, qkv, gates_t, gates_t)


def _mlstm_kernel(par_ref, qkf_ref, vf_ref, qkb_ref, vb_ref, gf_ref, gb_ref, of_ref, ob_ref,
                  cf_ref, cb_ref, mf_ref, mb_ref):
    i = pl.program_id(1)
    nch = qkf_ref.shape[1] // CHUNK

    @pl.when(i == 0)
    def _():
        for ref in (cf_ref, cb_ref, mf_ref, mb_ref):
            ref[...] = jnp.zeros_like(ref)

    ones_col = _bf(jnp.where(_iota2((CHUNK, ML_V_DIM), 1) == 0, 1.0, 0.0))

    items = []
    scans = []
    for hd in range(ML_HEADS):
        for qk_ref, v_ref, g_ref, o_ref, c_ref, m_ref, ch, rev in (
                (qkf_ref, vf_ref, gf_ref, of_ref, cf_ref, mf_ref, hd, False),
                (qkb_ref, vb_ref, gb_ref, ob_ref, cb_ref, mb_ref, ML_HEADS + hd, True)):
            incl, _, cum = _chunk_masks(rev)
            cum01 = _bf(jnp.where(cum, 1.0, 0.0))
            i_rows = g_ref[0, 4 * DN_HEADS + ch] + par_ref[4 * DN_HEADS + ch]
            f_rows = g_ref[0, 4 * DN_HEADS + 2 * ML_HEADS + ch] + par_ref[4 * DN_HEADS + 2 * ML_HEADS + ch]
            logf = -_softplus(-f_rows)
            bc_rows = _dot_exact_lhs(logf, cum01)
            b_tots = _dot_exact_lhs(logf, jnp.ones((CHUNK, LANES), BF16))
            lw_rows = b_tots[:, :CHUNK] - bc_rows + i_rows
            a_maxs = jnp.max(lw_rows, axis=-1, keepdims=True)
            wgt_rows = jnp.exp(lw_rows - a_maxs)
            run = jnp.concatenate([i_rows - bc_rows, jnp.full((nch, LANES - CHUNK), NEG_BIG, F32)], axis=1)
            live = _iota2((nch, LANES), 1) < CHUNK
            for s in (1, 2, 4, 8, 16, 32):
                run = jnp.where(live, jnp.maximum(run, pltpu.roll(run, (LANES - s) if rev else s, 1)), NEG_BIG)
            dmax_rows = bc_rows + run[:, :CHUNK]
            order = []
            for c in (range(nch - 1, -1, -1) if rev else range(nch)):
                one = slice(c, c + 1)
                it = dict(qk_ref=qk_ref, v_ref=v_ref, o_ref=o_ref, rows=slice(c * CHUNK, (c + 1) * CHUNK),
                          cols=slice(hd * ML_V_DIM, (hd + 1) * ML_V_DIM), incl=incl,
                          bc_row=bc_rows[one], i_row=i_rows[one], wgt_row=wgt_rows[one], b_tot=b_tots[one],
                          a_max=a_maxs[one], dmax_row=dmax_rows[one])
                items.append(it)
                order.append(it)
            scans.append((c_ref, m_ref, hd, order))

    def load_qkv(it):
        qk = it['qk_ref'][0, it['rows'], it['cols']]
        q = qk[:, :ML_QK_DIM]
        kf = qk[:, ML_QK_DIM:].astype(F32) * ML_QK_DIM ** -0.5
        v_ext = jnp.concatenate([it['v_ref'][0, it['rows'], it['cols']], ones_col], axis=1)
        return q, kf, v_ext

    eye_qk = _bf(jnp.where(_iota2((ML_QK_DIM, ML_QK_DIM), 0) == _iota2((ML_QK_DIM, ML_QK_DIM), 1), 1.0, 0.0))

    def bcast(it):
        it['bc_col'], it['dmax_col'] = _col_bcast2(it['bc_row'], it['dmax_row'])

    def transpose_k(it):
        _, kf, _ = load_qkv(it)
        it['k_t'] = _dot_nt(eye_qk, _bf(kf))

    def local(it):
        q, kf, v_ext = load_qkv(it)
        it['cc'] = _dot(_bf(it['k_t'] * it['wgt_row']), v_ext)
        it['qk'] = _dot_nt(q, _bf(kf))

    _each(items, transpose_k)
    _each(items, bcast)
    _each(items, local)

    m_states = [m_ref[hd] for _, m_ref, hd, _ in scans]
    for step in range(nch):
        for n, (_, _, _, order) in enumerate(scans):
            it = order[step]
            it['m_prev'] = m_states[n]
            m_states[n] = jnp.maximum(it['b_tot'] + it['m_prev'], it['a_max'])
            it['m_new'] = m_states[n]
    for n, (_, m_ref, hd, _) in enumerate(scans):
        m_ref[hd] = m_states[n]

    def rescale(it):
        sp = jnp.exp(it['b_tot'] + it['m_prev'] - it['m_new'])
        sc = jnp.exp(it['a_max'] - it['m_new'])
        it['sp'] = jnp.concatenate([sp, sp], axis=1)
        it['cc'] = it['cc'] * jnp.concatenate([sc, sc], axis=1)

    _each(items, rescale)
    c_states = [c_ref[hd] for c_ref, _, hd, _ in scans]
    for step in range(nch):
        for n, (_, _, _, order) in enumerate(scans):
            it = order[step]
            it['c_prev'] = _bf(c_states[n])
            c_states[n] = c_states[n] * it['sp'] + it['cc']
    for n, (c_ref, _, hd, _) in enumerate(scans):
        c_ref[hd] = c_states[n]

    def output(it):
        q, _, v_ext = load_qkv(it)
        inter_log = it['bc_col'] + it['m_prev']
        m_out = jnp.maximum(inter_log, it['dmax_col'])
        d_log = jnp.where(it['incl'], it['bc_col'][:, :CHUNK] - it['bc_row'] + it['i_row'], NEG_BIG)
        s = it['qk'] * jnp.exp(d_log - m_out[:, :CHUNK])
        inter = jnp.exp(inter_log - m_out)
        num = jnp.concatenate([inter, inter], axis=1) * _dot(q, it['c_prev']) + _dot(_bf(s), v_ext)
        den = num[:, ML_V_DIM:ML_V_DIM + 1]
        it['o_ref'][0, it['rows'], it['cols']] = num[:, :ML_V_DIM] / jnp.maximum(jnp.abs(den), jnp.exp(-m_out))

    _each(items, output)


def _mlstm(params, proj3, gates_t):
    nb, ns, _ = proj3.shape
    ts = SEQ_BLOCK
    nblk = ns // ts
    nch = ts // CHUNK
    qk_blk, v_blk = 5, 6

    def at(cblk, reverse):
        if reverse:
            return lambda b, i: (b, nblk - 1 - i, cblk)
        return lambda b, i: (b, i, cblk)

    blk = lambda cblk, reverse: pl.BlockSpec((1, ts, MIX_WIDTH), at(cblk, reverse))
    gspec = lambda reverse: pl.BlockSpec(
        (1, 32, nch, CHUNK), (lambda b, i: (b, 0, nblk - 1 - i, 0)) if reverse else (lambda b, i: (b, 0, i, 0)))
    out_sd = jax.ShapeDtypeStruct((nb, ns, MIX_WIDTH), F32)
    c_state = pltpu.VMEM((ML_HEADS, ML_QK_DIM, 2 * ML_V_DIM), F32)
    m_state = pltpu.VMEM((ML_HEADS, 1, LANES), F32)
    return pl.pallas_call(
        _mlstm_kernel, grid=(nb, nblk),
        in_specs=[pl.BlockSpec(memory_space=pltpu.SMEM),
                  blk(qk_blk, False), blk(v_blk, False), blk(qk_blk, True), blk(v_blk, True),
                  gspec(False), gspec(True)],
        out_specs=[blk(0, False), blk(0, True)],
        out_shape=[out_sd, out_sd],
        scratch_shapes=[c_state, c_state, m_state, m_state],
        compiler_params=_cparams(("parallel", "arbitrary")), name="mlstm",
    )(params, proj3, proj3, proj3, proj3, gates_t, gates_t)


def _pool_kernel(u_ref, up_ref, un_ref, w_ref, sc_ref, o_ref, xs_ref):
    i = pl.program_id(1)
    nblk = pl.num_programs(1)
    ts = u_ref.shape[1]
    _fill_halo(xs_ref, u_ref[0].astype(F32), up_ref[0].astype(F32), un_ref[0].astype(F32), i, nblk)
    pos = i * ts + _iota2((ts, 1), 0)
    seq = nblk * ts
    for g, win in enumerate(POOL_WINDOWS):
        half = win // 2
        cols = slice(g * POOL_DIM, (g + 1) * POOL_DIM)
        acc = jnp.zeros((ts, POOL_DIM), F32)
        for d in range(-half, half):
            acc = acc + xs_ref[8 + d:8 + d + ts, cols]
        cnt = (jnp.minimum(pos + half, seq) - jnp.maximum(pos - half, 0)).astype(F32)
        p = acc / cnt - xs_ref[8:8 + ts, cols]
        y = _dot(p.astype(BF16), w_ref[g]) * sc_ref[:, cols]
        o_ref[0, :, cols] = y.astype(BF16)


def _pool(proj3, pool_w, pool_scale):
    nb, ns, _ = proj3.shape
    ts = SEQ_BLOCK
    cblk = 4
    main, prev, make_next = _halo_specs(ts, MIX_WIDTH, lambda: cblk)
    nxt = make_next(ns // HALO)
    return pl.pallas_call(
        _pool_kernel, grid=(nb, ns // ts),
        in_specs=[pl.BlockSpec((1, ts, MIX_WIDTH), main), pl.BlockSpec((1, HALO, MIX_WIDTH), prev),
                  pl.BlockSpec((1, HALO, MIX_WIDTH), nxt),
                  pl.BlockSpec((len(POOL_WINDOWS), POOL_DIM, POOL_DIM), lambda b, i: (0, 0, 0)),
                  pl.BlockSpec((1, MIX_WIDTH), lambda b, i: (0, 0))],
        out_specs=pl.BlockSpec((1, ts, MIX_WIDTH), lambda b, i: (b, i, 0)),
        out_shape=jax.ShapeDtypeStruct((nb, ns, MIX_WIDTH), BF16),
        scratch_shapes=[pltpu.VMEM((ts + 16, MIX_WIDTH), F32)],
        compiler_params=_cparams(("parallel", "parallel")), name="pool",
    )(proj3, proj3, proj3, pool_w, pool_scale)


def _merge_kernel(x_ref, h_ref, of_ref, ob_ref, z_ref, yp_ref, mf_ref, mb_ref, mo_ref,
                  wg_ref, wb_ref, wo_ref, dnw_ref, mlw_ref, lg_ref, lb_ref, o_ref, obf_ref):
    o_dn = of_ref[...] + ob_ref[...]
    h_ml = mf_ref[...] + mb_ref[...]
    dn_parts = []
    ml_parts = []
    for hd in range(DN_HEADS):
        cols = slice(hd * DN_DIM, (hd + 1) * DN_DIM)
        a = o_dn[:, cols]
        dn_parts.append(a * lax.rsqrt(jnp.mean(a * a, -1, keepdims=True) + NORM_EPS))
        c = h_ml[:, cols]
        c = c - jnp.mean(c, -1, keepdims=True)
        ml_parts.append(c * lax.rsqrt(jnp.mean(c * c, -1, keepdims=True) + NORM_EPS))
    z = z_ref[...].astype(F32)
    y_dn = jnp.concatenate(dn_parts, axis=1) * dnw_ref[...] * (z * _sigmoid(z))
    y_ml = jnp.concatenate(ml_parts, axis=1) * mlw_ref[...] * _sigmoid(mo_ref[...].astype(F32))
    branches = (y_dn.astype(BF16), yp_ref[...], y_ml.astype(BF16))
    x = x_ref[...]
    merged = jnp.zeros(o_ref.shape, F32)
    for r in range(N_BRANCH):
        gate = _sigmoid(_dot(x, wg_ref[:, r * D_MODEL:(r + 1) * D_MODEL]))
        merged = merged + gate * _dot(branches[r], wb_ref[r])
    mix = _dot(merged.astype(BF16), wo_ref[...])
    y = _ln_rows(ALPHA * h_ref[...] + mix, lg_ref[...], lb_ref[...])
    o_ref[...] = y
    obf_ref[...] = y.astype(BF16)


def _merge(x_bf, h, o_f, o_b, proj, y_pool, m_f, m_b, w_gate, w_branch, w_out, dn_norm_w, ml_norm_w, ln_g, ln_b):
    t, d = h.shape
    tm = ROW_BLOCK
    row = lambda width, cblk=0: pl.BlockSpec((tm, width), lambda i: (i, cblk))
    const = lambda shape: pl.BlockSpec(shape, lambda i: (0,) * len(shape), pipeline_mode=pl.Buffered(1))
    return pl.pallas_call(
        _merge_kernel, grid=(t // tm,),
        in_specs=[row(d), row(d), row(MIX_WIDTH), row(MIX_WIDTH), row(MIX_WIDTH, 3), row(MIX_WIDTH),
                  row(MIX_WIDTH), row(MIX_WIDTH), row(MIX_WIDTH, 7),
                  const((d, N_BRANCH * d)), const((N_BRANCH, MIX_WIDTH, d)), const((d, d)),
                  const((1, MIX_WIDTH)), const((1, MIX_WIDTH)), const((1, d)), const((1, d))],
        out_specs=[row(d), row(d)],
        out_shape=[jax.ShapeDtypeStruct((t, d), F32), jax.ShapeDtypeStruct((t, d), BF16)],
        compiler_params=_cparams(("parallel",)), name="merge",
    )(x_bf, h, o_f, o_b, proj, y_pool, m_f, m_b, proj, w_gate, w_branch, w_out,
      dn_norm_w.reshape(1, -1), ml_norm_w.reshape(1, -1), ln_g.reshape(1, -1), ln_b.reshape(1, -1))


def _route_kernel(h_ref, rw_ref, rb_ref, e_ref, gw_ref, rank_ref, cnt_ref, base_ref):
    i = pl.program_id(0)
    tm = h_ref.shape[0]
    G, J = N_GROUPS, EXPERTS_PER_GROUP

    @pl.when(i == 0)
    def _():
        base_ref[...] = jnp.zeros_like(base_ref)

    hv = h_ref[...]
    h_hi = _bf(hv)
    h_lo = _bf(hv - h_hi.astype(F32))
    rw = rw_ref[...]
    rw_hi = _bf(rw)
    rw_lo = _bf(rw - rw_hi.astype(F32))
    pre = _dot(h_hi, rw_hi) + (_dot(h_lo, rw_hi) + _dot(h_hi, rw_lo))
    sel_rows = jnp.where(_iota2((N_EXPERTS, LANES), 0) == _iota2((N_EXPERTS, LANES), 1), 1.0, 0.0).astype(BF16)
    p1, p2, p3 = _split3(pre)
    pre_t = _dot_nt(sel_rows, p1) + (_dot_nt(sel_rows, p2) + _dot_nt(sel_rows, p3))
    score = [_sigmoid(pre_t[j * G:(j + 1) * G, :]) for j in range(J)]
    sel = [score[j] + rb_ref[j][:, :1] for j in range(J)]
    pair = None
    for a in range(J):
        for b in range(a + 1, J):
            s = sel[a] + sel[b]
            pair = s if pair is None else jnp.maximum(pair, s)
    gid = _iota2((G, tm), 0)
    gmax = jnp.max(pair, axis=0, keepdims=True)
    grp = jnp.min(jnp.where(pair == gmax, gid.astype(F32), float(G)), axis=0, keepdims=True).astype(jnp.int32)
    in_grp = gid == grp
    pick = lambda arr: jnp.sum(jnp.where(in_grp, arr, 0.0), axis=0, keepdims=True)
    v = [pick(sel[j]) for j in range(J)]
    u = [pick(score[j]) for j in range(J)]
    rank = []
    for j in range(J):
        rj = jnp.zeros((1, tm), jnp.int32)
        for a in range(J):
            if a == j:
                continue
            ahead = (v[a] >= v[j]) if a < j else (v[a] > v[j])
            rj = rj + ahead.astype(jnp.int32)
        rank.append(rj)
    loc = []
    wsel = []
    for kk in range(TOP_K):
        lk = jnp.zeros((1, tm), jnp.int32)
        wk = jnp.zeros((1, tm), F32)
        for j in range(J):
            hit = rank[j] == kk
            lk = lk + jnp.where(hit, j, 0)
            wk = wk + jnp.where(hit, u[j], 0.0)
        loc.append(lk)
        wsel.append(wk)
    wsum = wsel[0] + wsel[1]
    e_ref[...] = jnp.concatenate([grp * J + loc[0], grp * J + loc[1]], axis=0)
    gw_ref[...] = jnp.concatenate([wsel[0] / wsum, wsel[1] / wsum], axis=0)
    before = jnp.where(_iota2((tm, tm), 0) < _iota2((tm, tm), 1), 1.0, 0.0).astype(BF16)
    ones = jnp.ones((tm, LANES), BF16)
    r_k = [jnp.zeros((1, tm), F32) for _ in range(TOP_K)]
    for j in range(J):
        hits = [in_grp & (loc[kk] == j) for kk in range(TOP_K)]
        oh = jnp.where(hits[0] | hits[1], 1.0, 0.0)
        pos = _dot(oh.astype(BF16), before) + base_ref[j][:, :1]
        for kk in range(TOP_K):
            r_k[kk] = r_k[kk] + jnp.sum(jnp.where(hits[kk], pos, 0.0), axis=0, keepdims=True)
        base_ref[j] = base_ref[j] + _dot(oh.astype(BF16), ones)
    rank_ref[...] = jnp.concatenate(r_k, axis=0).astype(jnp.int32)
    cnt_ref[...] = base_ref[...]


def _route(h, router_w, router_b):
    t, d = h.shape
    tm = SEQ_BLOCK
    G, J = N_GROUPS, EXPERTS_PER_GROUP
    perm = np.asarray([g * J + j for j in range(J) for g in range(G)])
    rw = jnp.zeros((d, LANES), F32).at[:, :N_EXPERTS].set(router_w.astype(F32)[:, perm])
    rb = jnp.broadcast_to(router_b.astype(F32)[perm].reshape(J, G, 1), (J, G, LANES))
    pair = lambda dt: jax.ShapeDtypeStruct((TOP_K, t), dt)
    tok = pl.BlockSpec((TOP_K, tm), lambda i: (0, i))
    return pl.pallas_call(
        _route_kernel, grid=(t // tm,),
        in_specs=[pl.BlockSpec((tm, d), lambda i: (i, 0)), pl.BlockSpec((d, LANES), lambda i: (0, 0)),
                  pl.BlockSpec((J, G, LANES), lambda i: (0, 0, 0))],
        out_specs=[tok, tok, tok, pl.BlockSpec((J, G, LANES), lambda i: (0, 0, 0))],
        out_shape=[pair(jnp.int32), pair(F32), pair(jnp.int32), jax.ShapeDtypeStruct((J, G, LANES), F32)],
        scratch_shapes=[pltpu.VMEM((J, G, LANES), F32)],
        compiler_params=_cparams(("arbitrary",)), name="route",
    )(h, rw, rb)


def _dispatch_kernel(pend_ref, dest_ref, h_ref, xb_ref, zero_ref, sem, zsem):
    tm = h_ref.shape[0]

    @pl.when(pl.program_id(0) == 0)
    def _():
        zero_ref[...] = jnp.zeros_like(zero_ref)

        def fill(start):
            if not isinstance(start, int):
                start = pl.multiple_of(start, MOE_ROWS)
            return pltpu.make_async_copy(zero_ref, xb_ref.at[pl.ds(start, MOE_ROWS)], zsem)

        n_rows = xb_ref.shape[0]
        jobs = []
        for e in range(N_EXPERTS):
            jobs.append((pend_ref[e] > (pend_ref[e - 1] if e > 0 else 0), pend_ref[e] - MOE_ROWS))
            tail = n_rows - (e + 1) * MOE_ROWS
            jobs.append((tail >= pend_ref[N_EXPERTS - 1], tail))
        for cond, start in jobs:
            pl.when(cond)(lambda start=start: fill(start).start())
        for cond, start in jobs:
            pl.when(cond)(lambda start=start: fill(start).wait())

    def row_copy(t, kk):
        return pltpu.make_async_copy(h_ref.at[pl.ds(t, 1)], xb_ref.at[pl.ds(dest_ref[kk, t], 1)], sem)

    def start(t, carry):
        for kk in range(TOP_K):
            row_copy(t, kk).start(priority=kk % 2)
        return carry

    lax.fori_loop(0, tm, start, 0, unroll=8)
    for kk in range(TOP_K):
        pltpu.make_async_copy(h_ref, xb_ref.at[pl.ds(0, tm)], sem).wait()


def _dispatch(pend, dest, h, n_rows):
    t, d = h.shape
    tm = DISPATCH_BLOCK
    grid_spec = pltpu.PrefetchScalarGridSpec(
        num_scalar_prefetch=1, grid=(t // tm,),
        in_specs=[pl.BlockSpec((TOP_K, tm), lambda i, pe: (0, i), memory_space=pltpu.SMEM),
                  pl.BlockSpec((tm, d), lambda i, pe: (i, 0))],
        out_specs=pl.BlockSpec(memory_space=pl.ANY),
        scratch_shapes=[pltpu.VMEM((MOE_ROWS, d), F32), pltpu.SemaphoreType.DMA(()), pltpu.SemaphoreType.DMA(())])
    return pl.pallas_call(
        _dispatch_kernel, grid_spec=grid_spec, out_shape=jax.ShapeDtypeStruct((n_rows, d), F32),
        compiler_params=_cparams(("arbitrary",)), name="moe_dispatch",
    )(pend, dest, h)


def _expert_kernel(be_ref, nact_ref, x_ref, w1_ref, w3_ref, w2_ref, o_ref):
    i = pl.program_id(0)

    @pl.when(i < nact_ref[0])
    def _():
        x = x_ref[...].astype(BF16)
        a = _dot(x, w1_ref[0])
        b = _dot(x, w3_ref[0])
        act = a * _sigmoid(a) * b
        o_ref[...] = _dot(act.astype(BF16), w2_ref[0])

    @pl.when(i >= nact_ref[0])
    def _():
        o_ref[...] = jnp.zeros_like(o_ref)


def _experts(block_expert, n_active, xb, w1, w3, w2):
    n_rows, d = xb.shape
    bm = MOE_ROWS
    grid_spec = pltpu.PrefetchScalarGridSpec(
        num_scalar_prefetch=2, grid=(n_rows // bm,),
        in_specs=[pl.BlockSpec((bm, d), lambda i, be, na: (jnp.minimum(i, na[0] - 1), 0)),
                  pl.BlockSpec((1, d, D_EXPERT), lambda i, be, na: (be[i], 0, 0)),
                  pl.BlockSpec((1, d, D_EXPERT), lambda i, be, na: (be[i], 0, 0)),
                  pl.BlockSpec((1, D_EXPERT, d), lambda i, be, na: (be[i], 0, 0))],
        out_specs=pl.BlockSpec((bm, d), lambda i, be, na: (i, 0)))
    return pl.pallas_call(
        _expert_kernel, grid_spec=grid_spec, out_shape=jax.ShapeDtypeStruct((n_rows, d), F32),
        compiler_params=_cparams(("arbitrary",)), name="moe_experts",
    )(block_expert, n_active, xb, w1, w3, w2)


def _combine_kernel(dest_ref, dest_next_ref, h_ref, gw_ref, yb_ref, lg_ref, lb_ref, o_ref, obf_ref, buf_ref, sem):
    i = pl.program_id(0)
    tm = h_ref.shape[0]
    slot = lax.rem(i, 2)

    def gather(idx_ref, s):
        def start(t, carry):
            for kk in range(TOP_K):
                pltpu.make_async_copy(yb_ref.at[pl.ds(idx_ref[kk, t], 1)], buf_ref.at[s, kk, pl.ds(t, 1)],
                                      sem.at[s]).start(priority=kk % 2)
            return carry
        lax.fori_loop(0, tm, start, 0, unroll=8)

    @pl.when(i == 0)
    def _():
        gather(dest_ref, 0)

    @pl.when(i + 1 < pl.num_programs(0))
    def _():
        gather(dest_next_ref, 1 - slot)

    for kk in range(TOP_K):
        pltpu.make_async_copy(yb_ref.at[pl.ds(0, tm)], buf_ref.at[slot, kk], sem.at[slot]).wait()
    gw = gw_ref[...]
    ffn = gw[:, 0:1] * buf_ref[slot, 0] + gw[:, 1:2] * buf_ref[slot, 1]
    y = _ln_rows(ALPHA * h_ref[...] + ffn, lg_ref[...], lb_ref[...])
    o_ref[...] = y
    obf_ref[...] = y.astype(BF16)


def _combine(dest, h, gw_cols, yb, ln_g, ln_b):
    t, d = h.shape
    tm = COMBINE_BLOCK
    row = pl.BlockSpec((tm, d), lambda i: (i, 0))
    vec = pl.BlockSpec((1, d), lambda i: (0, 0))
    nsteps = t // tm
    return pl.pallas_call(
        _combine_kernel, grid=(nsteps,),
        in_specs=[pl.BlockSpec((TOP_K, tm), lambda i: (0, i), memory_space=pltpu.SMEM),
                  pl.BlockSpec((TOP_K, tm), lambda i: (0, jnp.minimum(i + 1, nsteps - 1)), memory_space=pltpu.SMEM),
                  row, pl.BlockSpec((tm, TOP_K), lambda i: (i, 0)), pl.BlockSpec(memory_space=pl.ANY), vec, vec],
        out_specs=[row, row],
        out_shape=[jax.ShapeDtypeStruct((t, d), F32), jax.ShapeDtypeStruct((t, d), BF16)],
        scratch_shapes=[pltpu.VMEM((2, TOP_K, tm, d), F32), pltpu.SemaphoreType.DMA((2,))],
        compiler_params=_cparams(("arbitrary",)), name="moe_combine",
    )(dest, dest, h, gw_cols, yb, ln_g.reshape(1, d), ln_b.reshape(1, d))


def _moe(h, router_w, router_b, w1, w3, w2, ln_g, ln_b):
    t, d = h.shape
    expert, gate_w, rank, counts = _route(h, router_w, router_b)
    cnt = counts[:, :, 0].T.reshape(N_EXPERTS).astype(jnp.int32)
    padded = ((cnt + MOE_ROWS - 1) // MOE_ROWS) * MOE_ROWS
    pend = jnp.cumsum(padded)
    pstart = pend - padded
    is_e = expert[..., None] == jnp.arange(N_EXPERTS, dtype=jnp.int32)
    dest = jnp.sum(jnp.where(is_e, pstart.astype(jnp.int32), 0), axis=-1) + rank
    n_rows = t * TOP_K + N_EXPERTS * MOE_ROWS
    n_blocks = n_rows // MOE_ROWS
    block_start = jnp.arange(n_blocks, dtype=jnp.int32) * MOE_ROWS
    block_expert = jnp.minimum(jnp.sum((pend[None, :] <= block_start[:, None]).astype(jnp.int32), axis=1), N_EXPERTS - 1)
    n_active = (pend[-1:] // MOE_ROWS).astype(jnp.int32)
    xb = _dispatch(pend.astype(jnp.int32), dest, h, n_rows)
    yb = _experts(block_expert, n_active, xb, w1, w3, w2)
    return _combine(dest, h, gate_w.T, yb, ln_g, ln_b)


def kernel(x, ln_in_g, ln_in_b, w_in, dn_conv_w, dn_a_log, dn_dt_bias, dn_norm_w, pool_w, pool_scale,
           ml_gate_b, ml_norm_w, w_branch, w_out, ln_mix_g, ln_mix_b, router_w, router_b,
           moe_w1, moe_w3, moe_w2, ln_ffn_g, ln_ffn_b):
    nb, ns, d = x.shape
    t = nb * ns
    h, h_bf = _layer_norm_in(x.reshape(t, d), ln_in_g, ln_in_b)
    for l in range(DEPTH):
        w_main, w_small, w_gate = _split_in_proj(w_in[l])
        proj = _matmul(h_bf, w_main, BF16, 512, 1024, "in_proj")
        small = _matmul(h_bf, w_small, F32, 512, LANES, "in_proj_gates")
        proj3 = proj.reshape(nb, ns, MAIN_WIDTH)
        gates_t = small[:, :32].reshape(nb, ns // CHUNK, CHUNK, 32).transpose(0, 3, 1, 2)
        params = jnp.concatenate([dn_a_log[l], dn_dt_bias[l], ml_gate_b[l]]).astype(F32)
        qkv = _dn_prep(proj3, dn_conv_w[l].astype(F32))
        o_f, o_b = _gdn(params, qkv, gates_t)
        y_pool = _pool(proj3, pool_w[l].astype(BF16), pool_scale[l].astype(F32).reshape(1, -1))
        m_f, m_b = _mlstm(params, proj3, gates_t)
        flat = lambda a: a.reshape(t, -1)
        h, h_bf = _merge(h_bf, h, flat(o_f), flat(o_b), proj, flat(y_pool), flat(m_f), flat(m_b),
                         w_gate, w_branch[l].astype(BF16), w_out[l].astype(BF16),
                         dn_norm_w[l].astype(F32), ml_norm_w[l].astype(F32), ln_mix_g[l], ln_mix_b[l])
        h, h_bf = _moe(h, router_w, router_b, moe_w1[l].astype(BF16), moe_w3[l].astype(BF16),
                       moe_w2[l].astype(BF16), ln_ffn_g[l], ln_ffn_b[l])
    return h.reshape(nb, ns, d)
```

```python
import jax
import jax.numpy as jnp
import numpy as np
from jax import lax
from jax.experimental import pallas as pl
from jax.experimental.pallas import tpu as pltpu

F32 = jnp.float32
BF16 = jnp.bfloat16

D_MODEL = 1024
DEPTH = 2
MIX_WIDTH = D_MODEL // 2
DN_HEADS = 4
DN_DIM = MIX_WIDTH // DN_HEADS
DN_CONV = 5
CHUNK = 64
POOL_WINDOWS = (2, 4, 8, 16)
POOL_DIM = MIX_WIDTH // len(POOL_WINDOWS)
ML_HEADS = 4
ML_V_DIM = MIX_WIDTH // ML_HEADS
ML_QK_DIM = ML_V_DIM // 2
N_BRANCH = 3
N_EXPERTS = 32
N_GROUPS = 8
EXPERTS_PER_GROUP = N_EXPERTS // N_GROUPS
TOP_K = 2
D_EXPERT = D_MODEL // 2
LN_EPS = 1e-5
NORM_EPS = 1e-6
ALPHA = (2.0 * DEPTH) ** 0.25

LANES = 128
HALO = 16
SEQ_BLOCK = 512
ROW_BLOCK = 512
MOE_ROWS = 256
DISPATCH_BLOCK = 512
COMBINE_BLOCK = 256
GDN_HEAD_GROUP = 4
NEG_BIG = -1e30
VMEM_LIMIT = 56 * 1024 * 1024

_SPLITS = (3 * MIX_WIDTH, MIX_WIDTH, 2 * DN_HEADS, 2 * DN_HEADS, MIX_WIDTH, ML_HEADS * ML_QK_DIM,
           ML_HEADS * ML_QK_DIM, MIX_WIDTH, MIX_WIDTH, 2 * ML_HEADS, 2 * ML_HEADS, N_BRANCH * D_MODEL)
_OFF = [0] + [int(c) for c in np.cumsum(_SPLITS)]
MAIN_WIDTH = 8 * MIX_WIDTH


def _split_in_proj(w):
    o = _OFF
    w = w.astype(BF16)
    parts = [w[:, o[0]:o[2]], w[:, o[4]:o[5]]]
    for h in range(ML_HEADS):
        parts.append(w[:, o[5] + h * ML_QK_DIM:o[5] + (h + 1) * ML_QK_DIM])
        parts.append(w[:, o[6] + h * ML_QK_DIM:o[6] + (h + 1) * ML_QK_DIM])
    parts.append(w[:, o[7]:o[9]])
    main = jnp.concatenate(parts, axis=1)
    n_small = (o[4] - o[2]) + (o[11] - o[9])
    small = jnp.concatenate([w[:, o[2]:o[4]], w[:, o[9]:o[11]], jnp.zeros((w.shape[0], LANES - n_small), BF16)], axis=1)
    return main, small, w[:, o[11]:o[12]]


def _cparams(sem, vmem=VMEM_LIMIT):
    return pltpu.CompilerParams(dimension_semantics=sem, vmem_limit_bytes=vmem)


def _dot(a, b):
    return jnp.dot(a, b, preferred_element_type=F32)


def _dot_nt(a, b):
    return lax.dot_general(a, b, (((1,), (1,)), ((), ())), preferred_element_type=F32)


def _dot_tn(a, b):
    return lax.dot_general(a, b, (((0,), (0,)), ((), ())), preferred_element_type=F32)


def _split3(a):
    a1 = a.astype(BF16)
    r1 = a - a1.astype(F32)
    a2 = r1.astype(BF16)
    a3 = (r1 - a2.astype(F32)).astype(BF16)
    return a1, a2, a3


def _dot_exact_lhs(a, b01):
    a1, a2, a3 = _split3(a)
    return _dot(a1, b01) + (_dot(a2, b01) + _dot(a3, b01))


def _sigmoid(x):
    return 1.0 / (1.0 + jnp.exp(-x))


def _softplus(x):
    return jnp.maximum(x, 0.0) + jnp.log(1.0 + jnp.exp(-jnp.abs(x)))


def _iota2(shape, dim):
    return lax.broadcasted_iota(jnp.int32, shape, dim)


def _ln_rows(x, g, b):
    mu = jnp.mean(x, -1, keepdims=True)
    xc = x - mu
    var = jnp.mean(xc * xc, -1, keepdims=True)
    return xc * lax.rsqrt(var + LN_EPS) * g + b


def _ln_kernel(x_ref, g_ref, b_ref, o_ref, obf_ref):
    y = _ln_rows(x_ref[...], g_ref[...], b_ref[...])
    o_ref[...] = y
    obf_ref[...] = y.astype(BF16)


def _layer_norm_in(x2, g, b):
    t, d = x2.shape
    tm = ROW_BLOCK
    row = pl.BlockSpec((tm, d), lambda i: (i, 0))
    vec = pl.BlockSpec((1, d), lambda i: (0, 0))
    return pl.pallas_call(
        _ln_kernel, grid=(t // tm,), in_specs=[row, vec, vec], out_specs=[row, row],
        out_shape=[jax.ShapeDtypeStruct((t, d), F32), jax.ShapeDtypeStruct((t, d), BF16)],
        compiler_params=_cparams(("parallel",)), name="ln_in",
    )(x2, g.reshape(1, d), b.reshape(1, d))


def _matmul_kernel(x_ref, w_ref, o_ref):
    o_ref[...] = _dot(x_ref[...], w_ref[...]).astype(o_ref.dtype)


def _matmul(x, w, out_dtype, tm, tn, name):
    t, k = x.shape
    n = w.shape[1]
    return pl.pallas_call(
        _matmul_kernel, grid=(n // tn, t // tm),
        in_specs=[pl.BlockSpec((tm, k), lambda j, i: (i, 0)), pl.BlockSpec((k, tn), lambda j, i: (0, j))],
        out_specs=pl.BlockSpec((tm, tn), lambda j, i: (i, j)),
        out_shape=jax.ShapeDtypeStruct((t, n), out_dtype),
        compiler_params=_cparams(("parallel", "parallel")), name=name,
    )(x, w)


def _fill_halo(xs_ref, main, prev, nxt, i, nblk):
    ts = main.shape[0]
    xs_ref[0:8, :] = jnp.where(i > 0, prev[HALO - 8:HALO, :], 0.0)
    xs_ref[8:8 + ts, :] = main
    xs_ref[8 + ts:16 + ts, :] = jnp.where(i < nblk - 1, nxt[0:8, :], 0.0)


def _dn_prep_kernel(x_ref, xp_ref, xn_ref, cw_ref, o_ref, xs_ref):
    i = pl.program_id(1)
    j = pl.program_id(2)
    ts = x_ref.shape[1]
    _fill_halo(xs_ref, x_ref[0].astype(F32), xp_ref[0].astype(F32), xn_ref[0].astype(F32), i, pl.num_programs(1))
    pad = DN_CONV // 2
    for hd in range(DN_HEADS):
        cols = slice(hd * DN_DIM, (hd + 1) * DN_DIM)
        acc = jnp.zeros((ts, DN_DIM), F32)
        for w in range(DN_CONV):
            acc = acc + xs_ref[8 + w - pad:8 + w - pad + ts, cols] * cw_ref[w:w + 1, cols]
        y = acc * _sigmoid(acc)
        inv = lax.rsqrt(jnp.sum(y * y, -1, keepdims=True) + NORM_EPS)
        scale = jnp.where(j == 0, inv * DN_DIM ** -0.5, jnp.where(j == 1, inv, 1.0))
        o_ref[0, :, cols] = (y * scale).astype(BF16)


def _halo_specs(ts, width, col_of):
    r = ts // HALO

    def main(b, i, *rest):
        return (b, i, col_of(*rest))

    def prev(b, i, *rest):
        return (b, jnp.maximum(i * r - 1, 0), col_of(*rest))

    def make_next(nrow_blocks):
        def nxt(b, i, *rest):
            return (b, jnp.minimum((i + 1) * r, nrow_blocks - 1), col_of(*rest))
        return nxt

    return main, prev, make_next


def _dn_prep(proj3, conv_w):
    nb, ns, _ = proj3.shape
    ts = SEQ_BLOCK
    main, prev, make_next = _halo_specs(ts, MIX_WIDTH, lambda j: j)
    nxt = make_next(ns // HALO)
    return pl.pallas_call(
        _dn_prep_kernel, grid=(nb, ns // ts, 3),
        in_specs=[pl.BlockSpec((1, ts, MIX_WIDTH), main), pl.BlockSpec((1, HALO, MIX_WIDTH), prev),
                  pl.BlockSpec((1, HALO, MIX_WIDTH), nxt), pl.BlockSpec((DN_CONV, MIX_WIDTH), lambda b, i, j: (0, j))],
        out_specs=pl.BlockSpec((1, ts, MIX_WIDTH), main),
        out_shape=jax.ShapeDtypeStruct((nb, ns, 3 * MIX_WIDTH), BF16),
        scratch_shapes=[pltpu.VMEM((ts + 16, MIX_WIDTH), F32)],
        compiler_params=_cparams(("parallel", "parallel", "parallel")), name="dn_prep",
    )(proj3, proj3, proj3, conv_w)


def _chunk_masks(rev):
    r = _iota2((CHUNK, CHUNK), 0)
    c = _iota2((CHUNK, CHUNK), 1)
    if rev:
        return r <= c, r < c, r >= c
    return r >= c, r > c, r <= c


def _col_bcast2(row_a, row_b):
    r = _iota2((CHUNK, CHUNK), 0)
    c = _iota2((CHUNK, CHUNK), 1)
    eye = r == c
    diag = jnp.concatenate([jnp.where(eye, jnp.broadcast_to(row_a, (CHUNK, CHUNK)), 0.0),
                            jnp.where(eye, jnp.broadcast_to(row_b, (CHUNK, CHUNK)), 0.0)], axis=1)
    rr = _iota2((2 * CHUNK, 2 * LANES), 0)
    cc = _iota2((2 * CHUNK, 2 * LANES), 1)
    ones_bd = jnp.where((rr // CHUNK) == (cc // LANES), 1.0, 0.0).astype(BF16)
    hi = _bf(diag)
    lo = _bf(diag - hi.astype(F32))
    out = _dot(hi, ones_bd) + _dot(lo, ones_bd)
    return out[:, :LANES], out[:, LANES:]


def _bf(x):
    return x.astype(BF16)


def _each(items, fn):
    for it in items:
        fn(it)


def _tri_inverse_stages():
    r = _iota2((CHUNK, CHUNK), 0)
    c = _iota2((CHUNK, CHUNK), 1)
    same16 = (r // 16) == (c // 16)
    same32 = (r // 32) == (c // 32)

    def start(it):
        it['md'] = jnp.where(same16, it['m'], 0.0)
        md_bf = _bf(it['md'])
        it['n'] = -it['md']
        it['a'] = _dot(md_bf, md_bf)

    def square(it):
        a_bf = _bf(it['a'])
        both = _dot(jnp.concatenate([_bf(it['n']), a_bf], axis=0), a_bf)
        it['n'] = it['n'] + it['a'] + both[:CHUNK]
        it['a'] = both[CHUNK:]

    def last_power(it):
        it['n'] = it['n'] + it['a'] + _dot(_bf(it['n']), _bf(it['a']))

    def eliminate_a(mask_of):
        def fn(it):
            cpl = mask_of(it)
            it['x'] = cpl + _dot(_bf(it['n']), _bf(cpl))
        return fn

    def eliminate_b(it):
        it['n'] = it['n'] - it['x'] - _dot(_bf(it['x']), _bf(it['n']))

    return [start, square, square, last_power,
            eliminate_a(lambda it: jnp.where(same32, it['m'] - it['md'], 0.0)), eliminate_b,
            eliminate_a(lambda it: jnp.where(same32, 0.0, it['m'])), eliminate_b]


def _gdn_kernel(par_ref, qf_ref, kf_ref, vf_ref, qb_ref, kb_ref, vb_ref, gf_ref, gb_ref,
                of_ref, ob_ref, sf_ref, sb_ref):
    i = pl.program_id(1)
    nch = qf_ref.shape[1] // CHUNK

    @pl.when(i == 0)
    def _():
        sf_ref[...] = jnp.zeros_like(sf_ref)
        sb_ref[...] = jnp.zeros_like(sb_ref)

    def build(hd):
        cols = slice(hd * DN_DIM, (hd + 1) * DN_DIM)
        items = []
        scans = []
        for q_ref, k_ref, v_ref, g_ref, o_ref, s_ref, ch, rev in (
                (qf_ref, kf_ref, vf_ref, gf_ref, of_ref, sf_ref, hd, False),
                (qb_ref, kb_ref, vb_ref, gb_ref, ob_ref, sb_ref, DN_HEADS + hd, True)):
            incl, strict, cum = _chunk_masks(rev)
            cum01 = _bf(jnp.where(cum, 1.0, 0.0))
            a_rows = g_ref[0, ch]
            b_rows = g_ref[0, 2 * DN_HEADS + ch]
            g_rows = (-jnp.exp(jnp.full((nch, CHUNK), par_ref[ch], F32))
                      * _softplus(a_rows + par_ref[2 * DN_HEADS + ch]))
            beta_rows = _sigmoid(b_rows)
            gc_rows = _dot_exact_lhs(g_rows, cum01)
            g_tots = _dot_exact_lhs(g_rows, jnp.ones((CHUNK, LANES), BF16))
            order = []
            for c in (range(nch - 1, -1, -1) if rev else range(nch)):
                rows = slice(c * CHUNK, (c + 1) * CHUNK)
                it = dict(q_ref=q_ref, k_ref=k_ref, v_ref=v_ref, o_ref=o_ref, rows=rows, cols=cols, incl=incl,
                          strict=strict, gc_row=gc_rows[c:c + 1], beta_row=beta_rows[c:c + 1], g_tot=g_tots[c:c + 1])
                items.append(it)
                order.append(it)
            scans.append(dict(s_ref=s_ref, hd=hd, order=order))
        return items, scans

    def bcast(it):
        it['gc_col'], it['beta_col'] = _col_bcast2(it['gc_row'], it['beta_row'])

    def gram(it):
        q = it['q_ref'][0, it['rows'], it['cols']]
        k = it['k_ref'][0, it['rows'], it['cols']]
        decay = jnp.exp(jnp.where(it['incl'], it['gc_col'][:, :CHUNK] - it['gc_row'], NEG_BIG))
        kb = k.astype(F32) * it['beta_col']
        both = _dot_nt(jnp.concatenate([_bf(kb), q], axis=0), k)
        it['m'] = jnp.where(it['strict'], both[:CHUNK] * decay, 0.0)
        it['attn'] = _bf(both[CHUNK:] * decay)

    def solve(it):
        q = it['q_ref'][0, it['rows'], it['cols']]
        kf = it['k_ref'][0, it['rows'], it['cols']].astype(F32)
        v = it['v_ref'][0, it['rows'], it['cols']]
        e_col = jnp.exp(it['gc_col'])
        rhs = jnp.concatenate([v.astype(F32) * it['beta_col'], kf * it['beta_col'] * e_col], axis=1)
        sol = rhs + _dot(_bf(it['n']), _bf(rhs))
        it['u'] = sol[:, :DN_DIM]
        it['wq'] = jnp.concatenate([_bf(sol[:, DN_DIM:]), _bf(q.astype(F32) * e_col)], axis=0)
        it['kdec'] = _bf(kf * jnp.exp(it['g_tot'] - it['gc_col']))
        it['e_tot'] = jnp.exp(it['g_tot'])

    stages =[bcast, gram] + _tri_inverse_stages() + [solve]
    assert len(stages) >= nch

    def scan_step(scans, step):
        for sc in scans:
            if step == 0:
                sc['state'] = sc['s_ref'][sc['hd']]
            it = sc['order'][step]
            sc['ws'] = _dot(it['wq'], _bf(sc['state']))
        for sc in scans:
            it = sc['order'][step]
            sc['v_new'] = _bf(it['u'] - sc['ws'][:CHUNK])
            sc['state'] = sc['state'] * it['e_tot'] + _dot_tn(it['kdec'], sc['v_new'])
        for sc in scans:
            it = sc['order'][step]
            it['o_ref'][0, it['rows'], it['cols']] = sc['ws'][CHUNK:] + _dot(it['attn'], sc['v_new'])
            if step == nch - 1:
                sc['s_ref'][sc['hd']] = sc['state']

    pending = None
    for first in range(0, DN_HEADS, GDN_HEAD_GROUP):
        items, scans = [], []
        for hd in range(first, first + GDN_HEAD_GROUP):
            head_items, head_scans = build(hd)
            items += head_items
            scans += head_scans
        for n_stage, stage in enumerate(stages):
            _each(items, stage)
            if pending is not None and n_stage < nch:
                scan_step(pending, n_stage)
        pending = scans
    for step in range(nch):
        scan_step(pending, step)


def _gdn(params, qkv, gates_t):
    nb, ns, _ = qkv.shape
    ts = SEQ_BLOCK
    nblk = ns // ts
    nch = ts // CHUNK

    def at(cblk, reverse):
        if reverse:
            return lambda b, i: (b, nblk - 1 - i, cblk)
        return lambda b, i: (b, i, cblk)

    blk = lambda cblk, reverse: pl.BlockSpec((1, ts, MIX_WIDTH), at(cblk, reverse))
    gspec = lambda reverse: pl.BlockSpec(
        (1, 32, nch, CHUNK), (lambda b, i: (b, 0, nblk - 1 - i, 0)) if reverse else (lambda b, i: (b, 0, i, 0)))
    out_sd = jax.ShapeDtypeStruct((nb, ns, MIX_WIDTH), F32)
    state = pltpu.VMEM((DN_HEADS, DN_DIM, DN_DIM), F32)
    return pl.pallas_call(
        _gdn_kernel, grid=(nb, nblk),
        in_specs=[pl.BlockSpec(memory_space=pltpu.SMEM),
                  blk(0, False), blk(1, False), blk(2, False), blk(0, True), blk(1, True), blk(2, True),
                  gspec(False), gspec(True)],
        out_specs=[blk(0, False), blk(0, True)],
        out_shape=[out_sd, out_sd],
        scratch_shapes=[state, state],
        compiler_params=_cparams(("parallel", "arbitrary")), name="gdn",
    )(params, qkv, qkv, qkv, qkv, qkv, qkv, gates_t, gates_t)


def _mlstm_kernel(par_ref, qkf_ref, vf_ref, qkb_ref, vb_ref, gf_ref, gb_ref, of_ref, ob_ref,
                  cf_ref, cb_ref, mf_ref, mb_ref):
    i = pl.program_id(1)
    nch = qkf_ref.shape[1] // CHUNK

    @pl.when(i == 0)
    def _():
        for ref in (cf_ref, cb_ref, mf_ref, mb_ref):
            ref[...] = jnp.zeros_like(ref)

    ones_col = _bf(jnp.where(_iota2((CHUNK, ML_V_DIM), 1) == 0, 1.0, 0.0))

    items = []
    scans = []
    for hd in range(ML_HEADS):
        for qk_ref, v_ref, g_ref, o_ref, c_ref, m_ref, ch, rev in (
                (qkf_ref, vf_ref, gf_ref, of_ref, cf_ref, mf_ref, hd, False),
                (qkb_ref, vb_ref, gb_ref, ob_ref, cb_ref, mb_ref, ML_HEADS + hd, True)):
            incl, _, cum = _chunk_masks(rev)
            cum01 = _bf(jnp.where(cum, 1.0, 0.0))
            i_rows = g_ref[0, 4 * DN_HEADS + ch] + par_ref[4 * DN_HEADS + ch]
            f_rows = g_ref[0, 4 * DN_HEADS + 2 * ML_HEADS + ch] + par_ref[4 * DN_HEADS + 2 * ML_HEADS + ch]
            logf = -_softplus(-f_rows)
            bc_rows = _dot_exact_lhs(logf, cum01)
            b_tots = _dot_exact_lhs(logf, jnp.ones((CHUNK, LANES), BF16))
            lw_rows = b_tots[:, :CHUNK] - bc_rows + i_rows
            a_maxs = jnp.max(lw_rows, axis=-1, keepdims=True)
            wgt_rows = jnp.exp(lw_rows - a_maxs)
            run = jnp.concatenate([i_rows - bc_rows, jnp.full((nch, LANES - CHUNK), NEG_BIG, F32)], axis=1)
            live = _iota2((nch, LANES), 1) < CHUNK
            for s in (1, 2, 4, 8, 16, 32):
                run = jnp.where(live, jnp.maximum(run, pltpu.roll(run, (LANES - s) if rev else s, 1)), NEG_BIG)
            dmax_rows = bc_rows + run[:, :CHUNK]
            order = []
            for c in (range(nch - 1, -1, -1) if rev else range(nch)):
                one = slice(c, c + 1)
                it = dict(qk_ref=qk_ref, v_ref=v_ref, o_ref=o_ref, rows=slice(c * CHUNK, (c + 1) * CHUNK),
                          cols=slice(hd * ML_V_DIM, (hd + 1) * ML_V_DIM), incl=incl,
                          bc_row=bc_rows[one], i_row=i_rows[one], wgt_row=wgt_rows[one], b_tot=b_tots[one],
                          a_max=a_maxs[one], dmax_row=dmax_rows[one])
                items.append(it)
                order.append(it)
            scans.append((c_ref, m_ref, hd, order))

    def load_qkv(it):
        qk = it['qk_ref'][0, it['rows'], it['cols']]
        q = qk[:, :ML_QK_DIM]
        kf = qk[:, ML_QK_DIM:].astype(F32) * ML_QK_DIM ** -0.5
        v_ext = jnp.concatenate([it['v_ref'][0, it['rows'], it['cols']], ones_col], axis=1)
        return q, kf, v_ext

    eye_qk = _bf(jnp.where(_iota2((ML_QK_DIM, ML_QK_DIM), 0) == _iota2((ML_QK_DIM, ML_QK_DIM), 1), 1.0, 0.0))

    def bcast(it):
        it['bc_col'], it['dmax_col'] = _col_bcast2(it['bc_row'], it['dmax_row'])

    def transpose_k(it):
        _, kf, _ = load_qkv(it)
        it['k_t'] = _dot_nt(eye_qk, _bf(kf))

    def local(it):
        q, kf, v_ext = load_qkv(it)
        it['cc'] = _dot(_bf(it['k_t'] * it['wgt_row']), v_ext)
        it['qk'] = _dot_nt(q, _bf(kf))

    _each(items, transpose_k)
    _each(items, bcast)
    _each(items, local)

    m_states = [m_ref[hd] for _, m_ref, hd, _ in scans]
    for step in range(nch):
        for n, (_, _, _, order) in enumerate(scans):
            it = order[step]
            it['m_prev'] = m_states[n]
            m_states[n] = jnp.maximum(it['b_tot'] + it['m_prev'], it['a_max'])
            it['m_new'] = m_states[n]
    for n, (_, m_ref, hd, _) in enumerate(scans):
        m_ref[hd] = m_states[n]

    def rescale(it):
        sp = jnp.exp(it['b_tot'] + it['m_prev'] - it['m_new'])
        sc = jnp.exp(it['a_max'] - it['m_new'])
        it['sp'] = jnp.concatenate([sp, sp], axis=1)
        it['cc'] = it['cc'] * jnp.concatenate([sc, sc], axis=1)

    _each(items, rescale)
    c_states = [c_ref[hd] for c_ref, _, hd, _ in scans]
    for step in range(nch):
        for n, (_, _, _, order) in enumerate(scans):
            it = order[step]
            it['c_prev'] = _bf(c_states[n])
            c_states[n] = c_states[n] * it['sp'] + it['cc']
    for n, (c_ref, _, hd, _) in enumerate(scans):
        c_ref[hd] = c_states[n]

    def output(it):
        q, _, v_ext = load_qkv(it)
        inter_log = it['bc_col'] + it['m_prev']
        m_out = jnp.maximum(inter_log, it['dmax_col'])
        d_log = jnp.where(it['incl'], it['bc_col'][:, :CHUNK] - it['bc_row'] + it['i_row'], NEG_BIG)
        s = it['qk'] * jnp.exp(d_log - m_out[:, :CHUNK])
        inter = jnp.exp(inter_log - m_out)
        num = jnp.concatenate([inter, inter], axis=1) * _dot(q, it['c_prev']) + _dot(_bf(s), v_ext)
        den = num[:, ML_V_DIM:ML_V_DIM + 1]
        it['o_ref'][0, it['rows'], it['cols']] = num[:, :ML_V_DIM] / jnp.maximum(jnp.abs(den), jnp.exp(-m_out))

    _each(items, output)


def _mlstm(params, proj3, gates_t):
    nb, ns, _ = proj3.shape
    ts = SEQ_BLOCK
    nblk = ns // ts
    nch = ts // CHUNK
    qk_blk, v_blk = 5, 6

    def at(cblk, reverse):
        if reverse:
            return lambda b, i: (b, nblk - 1 - i, cblk)
        return lambda b, i: (b, i, cblk)

    blk = lambda cblk, reverse: pl.BlockSpec((1, ts, MIX_WIDTH), at(cblk, reverse))
    gspec = lambda reverse: pl.BlockSpec(
        (1, 32, nch, CHUNK), (lambda b, i: (b, 0, nblk - 1 - i, 0)) if reverse else (lambda b, i: (b, 0, i, 0)))
    out_sd = jax.ShapeDtypeStruct((nb, ns, MIX_WIDTH), F32)
    c_state = pltpu.VMEM((ML_HEADS, ML_QK_DIM, 2 * ML_V_DIM), F32)
    m_state = pltpu.VMEM((ML_HEADS, 1, LANES), F32)
    return pl.pallas_call(
        _mlstm_kernel, grid=(nb, nblk),
        in_specs=[pl.BlockSpec(memory_space=pltpu.SMEM),
                  blk(qk_blk, False), blk(v_blk, False), blk(qk_blk, True), blk(v_blk, True),
                  gspec(False), gspec(True)],
        out_specs=[blk(0, False), blk(0, True)],
        out_shape=[out_sd, out_sd],
        scratch_shapes=[c_state, c_state, m_state, m_state],
        compiler_params=_cparams(("parallel", "arbitrary")), name="mlstm",
    )(params, proj3, proj3, proj3, proj3, gates_t, gates_t)


def _pool_kernel(u_ref, up_ref, un_ref, w_ref, sc_ref, o_ref, xs_ref):
    i = pl.program_id(1)
    nblk = pl.num_programs(1)
    ts = u_ref.shape[1]
    _fill_halo(xs_ref, u_ref[0].astype(F32), up_ref[0].astype(F32), un_ref[0].astype(F32), i, nblk)
    pos = i * ts + _iota2((ts, 1), 0)
    seq = nblk * ts
    for g, win in enumerate(POOL_WINDOWS):
        half = win // 2
        cols = slice(g * POOL_DIM, (g + 1) * POOL_DIM)
        acc = jnp.zeros((ts, POOL_DIM), F32)
        for d in range(-half, half):
            acc = acc + xs_ref[8 + d:8 + d + ts, cols]
        cnt = (jnp.minimum(pos + half, seq) - jnp.maximum(pos - half, 0)).astype(F32)
        p = acc / cnt - xs_ref[8:8 + ts, cols]
        y = _dot(p.astype(BF16), w_ref[g]) * sc_ref[:, cols]
        o_ref[0, :, cols] = y.astype(BF16)


def _pool(proj3, pool_w, pool_scale):
    nb, ns, _ = proj3.shape
    ts = SEQ_BLOCK
    cblk = 4
    main, prev, make_next = _halo_specs(ts, MIX_WIDTH, lambda: cblk)
    nxt = make_next(ns // HALO)
    return pl.pallas_call(
        _pool_kernel, grid=(nb, ns // ts),
        in_specs=[pl.BlockSpec((1, ts, MIX_WIDTH), main), pl.BlockSpec((1, HALO, MIX_WIDTH), prev),
                  pl.BlockSpec((1, HALO, MIX_WIDTH), nxt),
                  pl.BlockSpec((len(POOL_WINDOWS), POOL_DIM, POOL_DIM), lambda b, i: (0, 0, 0)),
                  pl.BlockSpec((1, MIX_WIDTH), lambda b, i: (0, 0))],
        out_specs=pl.BlockSpec((1, ts, MIX_WIDTH), lambda b, i: (b, i, 0)),
        out_shape=jax.ShapeDtypeStruct((nb, ns, MIX_WIDTH), BF16),
        scratch_shapes=[pltpu.VMEM((ts + 16, MIX_WIDTH), F32)],
        compiler_params=_cparams(("parallel", "parallel")), name="pool",
    )(proj3, proj3, proj3, pool_w, pool_scale)


def _merge_kernel(x_ref, h_ref, of_ref, ob_ref, z_ref, yp_ref, mf_ref, mb_ref, mo_ref,
                  wg_ref, wb_ref, wo_ref, dnw_ref, mlw_ref, lg_ref, lb_ref, o_ref, obf_ref):
    o_dn = of_ref[...] + ob_ref[...]
    h_ml = mf_ref[...] + mb_ref[...]
    dn_parts = []
    ml_parts = []
    for hd in range(DN_HEADS):
        cols = slice(hd * DN_DIM, (hd + 1) * DN_DIM)
        a = o_dn[:, cols]
        dn_parts.append(a * lax.rsqrt(jnp.mean(a * a, -1, keepdims=True) + NORM_EPS))
        c = h_ml[:, cols]
        c = c - jnp.mean(c, -1, keepdims=True)
        ml_parts.append(c * lax.rsqrt(jnp.mean(c * c, -1, keepdims=True) + NORM_EPS))
    z = z_ref[...].astype(F32)
    y_dn = jnp.concatenate(dn_parts, axis=1) * dnw_ref[...] * (z * _sigmoid(z))
    y_ml = jnp.concatenate(ml_parts, axis=1) * mlw_ref[...] * _sigmoid(mo_ref[...].astype(F32))
    branches = (y_dn.astype(BF16), yp_ref[...], y_ml.astype(BF16))
    x = x_ref[...]
    merged = jnp.zeros(o_ref.shape, F32)
    for r in range(N_BRANCH):
        gate = _sigmoid(_dot(x, wg_ref[:, r * D_MODEL:(r + 1) * D_MODEL]))
        merged = merged + gate * _dot(branches[r], wb_ref[r])
    mix = _dot(merged.astype(BF16), wo_ref[...])
    y = _ln_rows(ALPHA * h_ref[...] + mix, lg_ref[...], lb_ref[...])
    o_ref[...] = y
    obf_ref[...] = y.astype(BF16)


def _merge(x_bf, h, o_f, o_b, proj, y_pool, m_f, m_b, w_gate, w_branch, w_out, dn_norm_w, ml_norm_w, ln_g, ln_b):
    t, d = h.shape
    tm = ROW_BLOCK
    row = lambda width, cblk=0: pl.BlockSpec((tm, width), lambda i: (i, cblk))
    const = lambda shape: pl.BlockSpec(shape, lambda i: (0,) * len(shape), pipeline_mode=pl.Buffered(1))
    return pl.pallas_call(
        _merge_kernel, grid=(t // tm,),
        in_specs=[row(d), row(d), row(MIX_WIDTH), row(MIX_WIDTH), row(MIX_WIDTH, 3), row(MIX_WIDTH),
                  row(MIX_WIDTH), row(MIX_WIDTH), row(MIX_WIDTH, 7),
                  const((d, N_BRANCH * d)), const((N_BRANCH, MIX_WIDTH, d)), const((d, d)),
                  const((1, MIX_WIDTH)), const((1, MIX_WIDTH)), const((1, d)), const((1, d))],
        out_specs=[row(d), row(d)],
        out_shape=[jax.ShapeDtypeStruct((t, d), F32), jax.ShapeDtypeStruct((t, d), BF16)],
        compiler_params=_cparams(("parallel",)), name="merge",
    )(x_bf, h, o_f, o_b, proj, y_pool, m_f, m_b, proj, w_gate, w_branch, w_out,
      dn_norm_w.reshape(1, -1), ml_norm_w.reshape(1, -1), ln_g.reshape(1, -1), ln_b.reshape(1, -1))


def _route_kernel(h_ref, rw_ref, rb_ref, e_ref, gw_ref, rank_ref, cnt_ref, base_ref):
    i = pl.program_id(0)
    tm = h_ref.shape[0]
    G, J = N_GROUPS, EXPERTS_PER_GROUP

    @pl.when(i == 0)
    def _():
        base_ref[...] = jnp.zeros_like(base_ref)

    hv = h_ref[...]
    h_hi = _bf(hv)
    h_lo = _bf(hv - h_hi.astype(F32))
    rw = rw_ref[...]
    rw_hi = _bf(rw)
    rw_lo = _bf(rw - rw_hi.astype(F32))
    pre = _dot(h_hi, rw_hi) + (_dot(h_lo, rw_hi) + _dot(h_hi, rw_lo))
    sel_rows = jnp.where(_iota2((N_EXPERTS, LANES), 0) == _iota2((N_EXPERTS, LANES), 1), 1.0, 0.0).astype(BF16)
    p1, p2, p3 = _split3(pre)
    pre_t = _dot_nt(sel_rows, p1) + (_dot_nt(sel_rows, p2) + _dot_nt(sel_rows, p3))
    score = [_sigmoid(pre_t[j * G:(j + 1) * G, :]) for j in range(J)]
    sel = [score[j] + rb_ref[j][:, :1] for j in range(J)]
    pair = None
    for a in range(J):
        for b in range(a + 1, J):
            s = sel[a] + sel[b]
            pair = s if pair is None else jnp.maximum(pair, s)
    gid = _iota2((G, tm), 0)
    gmax = jnp.max(pair, axis=0, keepdims=True)
    grp = jnp.min(jnp.where(pair == gmax, gid.astype(F32), float(G)), axis=0, keepdims=True).astype(jnp.int32)
    in_grp = gid == grp
    pick = lambda arr: jnp.sum(jnp.where(in_grp, arr, 0.0), axis=0, keepdims=True)
    v = [pick(sel[j]) for j in range(J)]
    u = [pick(score[j]) for j in range(J)]
    rank = []
    for j in range(J):
        rj = jnp.zeros((1, tm), jnp.int32)
        for a in range(J):
            if a == j:
                continue
            ahead = (v[a] >= v[j]) if a < j else (v[a] > v[j])
            rj = rj + ahead.astype(jnp.int32)
        rank.append(rj)
    loc = []
    wsel = []
    for kk in range(TOP_K):
        lk = jnp.zeros((1, tm), jnp.int32)
        wk = jnp.zeros((1, tm), F32)
        for j in range(J):
            hit = rank[j] == kk
            lk = lk + jnp.where(hit, j, 0)
            wk = wk + jnp.where(hit, u[j], 0.0)
        loc.append(lk)
        wsel.append(wk)
    wsum = wsel[0] + wsel[1]
    e_ref[...] = jnp.concatenate([grp * J + loc[0], grp * J + loc[1]], axis=0)
    gw_ref[...] = jnp.concatenate([wsel[0] / wsum, wsel[1] / wsum], axis=0)
    before = jnp.where(_iota2((tm, tm), 0) < _iota2((tm, tm), 1), 1.0, 0.0).astype(BF16)
    ones = jnp.ones((tm, LANES), BF16)
    r_k = [jnp.zeros((1, tm), F32) for _ in range(TOP_K)]
    for j in range(J):
        hits = [in_grp & (loc[kk] == j) for kk in range(TOP_K)]
        oh = jnp.where(hits[0] | hits[1], 1.0, 0.0)
        pos = _dot(oh.astype(BF16), before) + base_ref[j][:, :1]
        for kk in range(TOP_K):
            r_k[kk] = r_k[kk] + jnp.sum(jnp.where(hits[kk], pos, 0.0), axis=0, keepdims=True)
        base_ref[j] = base_ref[j] + _dot(oh.astype(BF16), ones)
    rank_ref[...] = jnp.concatenate(r_k, axis=0).astype(jnp.int32)
    cnt_ref[...] = base_ref[...]


def _route(h, router_w, router_b):
    t, d = h.shape
    tm = SEQ_BLOCK
    G, J = N_GROUPS, EXPERTS_PER_GROUP
    perm = np.asarray([g * J + j for j in range(J) for g in range(G)])
    rw = jnp.zeros((d, LANES), F32).at[:, :N_EXPERTS].set(router_w.astype(F32)[:, perm])
    rb = jnp.broadcast_to(router_b.astype(F32)[perm].reshape(J, G, 1), (J, G, LANES))
    pair = lambda dt: jax.ShapeDtypeStruct((TOP_K, t), dt)
    tok = pl.BlockSpec((TOP_K, tm), lambda i: (0, i))
    return pl.pallas_call(
        _route_kernel, grid=(t // tm,),
        in_specs=[pl.BlockSpec((tm, d), lambda i: (i, 0)), pl.BlockSpec((d, LANES), lambda i: (0, 0)),
                  pl.BlockSpec((J, G, LANES), lambda i: (0, 0, 0))],
        out_specs=[tok, tok, tok, pl.BlockSpec((J, G, LANES), lambda i: (0, 0, 0))],
        out_shape=[pair(jnp.int32), pair(F32), pair(jnp.int32), jax.ShapeDtypeStruct((J, G, LANES), F32)],
        scratch_shapes=[pltpu.VMEM((J, G, LANES), F32)],
        compiler_params=_cparams(("arbitrary",)), name="route",
    )(h, rw, rb)


def _dispatch_kernel(pend_ref, dest_ref, h_ref, xb_ref, zero_ref, sem, zsem):
    tm = h_ref.shape[0]

    @pl.when(pl.program_id(0) == 0)
    def _():
        zero_ref[...] = jnp.zeros_like(zero_ref)

        def fill(start):
            if not isinstance(start, int):
                start = pl.multiple_of(start, MOE_ROWS)
            return pltpu.make_async_copy(zero_ref, xb_ref.at[pl.ds(start, MOE_ROWS)], zsem)

        n_rows = xb_ref.shape[0]
        jobs = []
        for e in range(N_EXPERTS):
            jobs.append((pend_ref[e] > (pend_ref[e - 1] if e > 0 else 0), pend_ref[e] - MOE_ROWS))
            tail = n_rows - (e + 1) * MOE_ROWS
            jobs.append((tail >= pend_ref[N_EXPERTS - 1], tail))
        for cond, start in jobs:
            pl.when(cond)(lambda start=start: fill(start).start())
        for cond, start in jobs:
            pl.when(cond)(lambda start=start: fill(start).wait())

    def row_copy(t, kk):
        return pltpu.make_async_copy(h_ref.at[pl.ds(t, 1)], xb_ref.at[pl.ds(dest_ref[kk, t], 1)], sem)

    def start(t, carry):
        for kk in range(TOP_K):
            row_copy(t, kk).start(priority=kk % 2)
        return carry

    lax.fori_loop(0, tm, start, 0, unroll=8)
    for kk in range(TOP_K):
        pltpu.make_async_copy(h_ref, xb_ref.at[pl.ds(0, tm)], sem).wait()


def _dispatch(pend, dest, h, n_rows):
    t, d = h.shape
    tm = DISPATCH_BLOCK
    grid_spec = pltpu.PrefetchScalarGridSpec(
        num_scalar_prefetch=1, grid=(t // tm,),
        in_specs=[pl.BlockSpec((TOP_K, tm), lambda i, pe: (0, i), memory_space=pltpu.SMEM),
                  pl.BlockSpec((tm, d), lambda i, pe: (i, 0))],
        out_specs=pl.BlockSpec(memory_space=pl.ANY),
        scratch_shapes=[pltpu.VMEM((MOE_ROWS, d), F32), pltpu.SemaphoreType.DMA(()), pltpu.SemaphoreType.DMA(())])
    return pl.pallas_call(
        _dispatch_kernel, grid_spec=grid_spec, out_shape=jax.ShapeDtypeStruct((n_rows, d), F32),
        compiler_params=_cparams(("arbitrary",)), name="moe_dispatch",
    )(pend, dest, h)


def _expert_kernel(be_ref, nact_ref, x_ref, w1_ref, w3_ref, w2_ref, o_ref):
    i = pl.program_id(0)

    @pl.when(i < nact_ref[0])
    def _():
        x = x_ref[...].astype(BF16)
        a = _dot(x, w1_ref[0])
        b = _dot(x, w3_ref[0])
        act = a * _sigmoid(a) * b
        o_ref[...] = _dot(act.astype(BF16), w2_ref[0])

    @pl.when(i >= nact_ref[0])
    def _():
        o_ref[...] = jnp.zeros_like(o_ref)


def _experts(block_expert, n_active, xb, w1, w3, w2):
    n_rows, d = xb.shape
    bm = MOE_ROWS
    grid_spec = pltpu.PrefetchScalarGridSpec(
        num_scalar_prefetch=2, grid=(n_rows // bm,),
        in_specs=[pl.BlockSpec((bm, d), lambda i, be, na: (jnp.minimum(i, na[0] - 1), 0)),
                  pl.BlockSpec((1, d, D_EXPERT), lambda i, be, na: (be[i], 0, 0)),
                  pl.BlockSpec((1, d, D_EXPERT), lambda i, be, na: (be[i], 0, 0)),
                  pl.BlockSpec((1, D_EXPERT, d), lambda i, be, na: (be[i], 0, 0))],
        out_specs=pl.BlockSpec((bm, d), lambda i, be, na: (i, 0)))
    return pl.pallas_call(
        _expert_kernel, grid_spec=grid_spec, out_shape=jax.ShapeDtypeStruct((n_rows, d), F32),
        compiler_params=_cparams(("arbitrary",)), name="moe_experts",
    )(block_expert, n_active, xb, w1, w3, w2)


def _combine_kernel(dest_ref, dest_next_ref, h_ref, gw_ref, yb_ref, lg_ref, lb_ref, o_ref, obf_ref, buf_ref, sem):
    i = pl.program_id(0)
    tm = h_ref.shape[0]
    slot = lax.rem(i, 2)

    def gather(idx_ref, s):
        def start(t, carry):
            for kk in range(TOP_K):
                pltpu.make_async_copy(yb_ref.at[pl.ds(idx_ref[kk, t], 1)], buf_ref.at[s, kk, pl.ds(t, 1)],
                                      sem.at[s]).start(priority=kk % 2)
            return carry
        lax.fori_loop(0, tm, start, 0, unroll=8)

    @pl.when(i == 0)
    def _():
        gather(dest_ref, 0)

    @pl.when(i + 1 < pl.num_programs(0))
    def _():
        gather(dest_next_ref, 1 - slot)

    for kk in range(TOP_K):
        pltpu.make_async_copy(yb_ref.at[pl.ds(0, tm)], buf_ref.at[slot, kk], sem.at[slot]).wait()
    gw = gw_ref[...]
    ffn = gw[:, 0:1] * buf_ref[slot, 0] + gw[:, 1:2] * buf_ref[slot, 1]
    y = _ln_rows(ALPHA * h_ref[...] + ffn, lg_ref[...], lb_ref[...])
    o_ref[...] = y
    obf_ref[...] = y.astype(BF16)


def _combine(dest, h, gw_cols, yb, ln_g, ln_b):
    t, d = h.shape
    tm = COMBINE_BLOCK
    row = pl.BlockSpec((tm, d), lambda i: (i, 0))
    vec = pl.BlockSpec((1, d), lambda i: (0, 0))
    nsteps = t // tm
    return pl.pallas_call(
        _combine_kernel, grid=(nsteps,),
        in_specs=[pl.BlockSpec((TOP_K, tm), lambda i: (0, i), memory_space=pltpu.SMEM),
                  pl.BlockSpec((TOP_K, tm), lambda i: (0, jnp.minimum(i + 1, nsteps - 1)), memory_space=pltpu.SMEM),
                  row, pl.BlockSpec((tm, TOP_K), lambda i: (i, 0)), pl.BlockSpec(memory_space=pl.ANY), vec, vec],
        out_specs=[row, row],
        out_shape=[jax.ShapeDtypeStruct((t, d), F32), jax.ShapeDtypeStruct((t, d), BF16)],
        scratch_shapes=[pltpu.VMEM((2, TOP_K, tm, d), F32), pltpu.SemaphoreType.DMA((2,))],
        compiler_params=_cparams(("arbitrary",)), name="moe_combine",
    )(dest, dest, h, gw_cols, yb, ln_g.reshape(1, d), ln_b.reshape(1, d))


def _moe(h, router_w, router_b, w1, w3, w2, ln_g, ln_b):
    t, d = h.shape
    expert, gate_w, rank, counts = _route(h, router_w, router_b)
    cnt = counts[:, :, 0].T.reshape(N_EXPERTS).astype(jnp.int32)
    padded = ((cnt + MOE_ROWS - 1) // MOE_ROWS) * MOE_ROWS
    pend = jnp.cumsum(padded)
    pstart = pend - padded
    is_e = expert[..., None] == jnp.arange(N_EXPERTS, dtype=jnp.int32)
    dest = jnp.sum(jnp.where(is_e, pstart.astype(jnp.int32), 0), axis=-1) + rank
    n_rows = t * TOP_K + N_EXPERTS * MOE_ROWS
    n_blocks = n_rows // MOE_ROWS
    block_start = jnp.arange(n_blocks, dtype=jnp.int32) * MOE_ROWS
    block_expert = jnp.minimum(jnp.sum((pend[None, :] <= block_start[:, None]).astype(jnp.int32), axis=1), N_EXPERTS - 1)
    n_active = (pend[-1:] // MOE_ROWS).astype(jnp.int32)
    xb = _dispatch(pend.astype(jnp.int32), dest, h, n_rows)
    yb = _experts(block_expert, n_active, xb, w1, w3, w2)
    return _combine(dest, h, gate_w.T, yb, ln_g, ln_b)


def kernel(x, ln_in_g, ln_in_b, w_in, dn_conv_w, dn_a_log, dn_dt_bias, dn_norm_w, pool_w, pool_scale,
           ml_gate_b, ml_norm_w, w_branch, w_out, ln_mix_g, ln_mix_b, router_w, router_b,
           moe_w1, moe_w3, moe_w2, ln_ffn_g, ln_ffn_b):
    nb, ns, d = x.shape
    t = nb * ns
    h, h_bf = _layer_norm_in(x.reshape(t, d), ln_in_g, ln_in_b)
    for l in range(DEPTH):
        w_main, w_small, w_gate = _split_in_proj(w_in[l])
        proj = _matmul(h_bf, w_main, BF16, 512, 1024, "in_proj")
        small = _matmul(h_bf, w_small, F32, 512, LANES, "in_proj_gates")
        proj3 = proj.reshape(nb, ns, MAIN_WIDTH)
        gates_t = small[:, :32].reshape(nb, ns // CHUNK, CHUNK, 32).transpose(0, 3, 1, 2)
        params = jnp.concatenate([dn_a_log[l], dn_dt_bias[l], ml_gate_b[l]]).astype(F32)
        qkv = _dn_prep(proj3, dn_conv_w[l].astype(F32))
        o_f, o_b = _gdn(params, qkv, gates_t)
        y_pool = _pool(proj3, pool_w[l].astype(BF16), pool_scale[l].astype(F32).reshape(1, -1))
        m_f, m_b = _mlstm(params, proj3, gates_t)
        flat = lambda a: a.reshape(t, -1)
        h, h_bf = _merge(h_bf, h, flat(o_f), flat(o_b), proj, flat(y_pool), flat(m_f), flat(m_b),
                         w_gate, w_branch[l].astype(BF16), w_out[l].astype(BF16),
                         dn_norm_w[l].astype(F32), ml_norm_w[l].astype(F32), ln_mix_g[l], ln_mix_b[l])
        h, h_bf = _moe(h, router_w, router_b, moe_w1[l].astype(BF16), moe_w3[l].astype(BF16),
                       moe_w2[l].astype(BF16), ln_ffn_g[l], ln_ffn_b[l])
    return h.reshape(nb, ns, d)
```
